```python
import jax
import jax.numpy as jnp
from jax import lax
import numpy as np

D_MODEL = 1024
BATCH = 8
SEQ = 2048
DEPTH = 4
DEC_BATCH = 128
DEC_SEQ = 4
PAST_LEN = 2048
PAGE_SIZE = 128

N_MIXERS = 3
N_HEADS = 16
HEAD_DIM = D_MODEL // N_HEADS
HD = N_HEADS * HEAD_DIM
N_KV = 4
KVD = 2 * N_KV * HEAD_DIM
ROPE_THETA = 10000.0
RMS_EPS = 1e-6
Q_BLOCK = 128

NSA_BLOCK = 64
NSA_TOP_N = 16
NSA_WINDOW = 512
NSA_Q_BLOCK = 64
NSA_PROJ = HD + 3 * KVD + 3 * N_HEADS + HD

FORGET_BIAS = 3.0
FOX_PROJ = 3 * HD + N_HEADS + HD

IDX_HEADS = 8
IDX_DIM = 64
DSA_TOP_K = 256
DSA_PROJ = HD + KVD + IDX_HEADS * IDX_DIM + IDX_DIM + IDX_HEADS + HD

N_NSA = (DEPTH + 2) // 3
N_FOX = (DEPTH + 1) // 3
N_DSA = DEPTH // 3

kernel_name = 'hybrid_nsa_fox_dsa_decoder_step'

F32 = jnp.float32


def rms_norm(x, g):
    xf = x.astype(F32)
    y = xf * lax.rsqrt(jnp.mean(xf * xf, axis=-1, keepdims=True) + RMS_EPS)
    return (y * g.astype(F32)).astype(x.dtype)


def rope(x, pos):
    half = x.shape[-1] // 2
    inv_freq = ROPE_THETA ** (-jnp.arange(half, dtype=F32) / half)
    ang = pos.astype(F32)[:, None] * inv_freq[None, :]
    cos = jnp.cos(ang)[None, :, None, :]
    sin = jnp.sin(ang)[None, :, None, :]
    xf = x.astype(F32)
    x1, x2 = xf[..., :half], xf[..., half:]
    return jnp.concatenate([x1 * cos - x2 * sin, x2 * cos + x1 * sin], axis=-1).astype(x.dtype)


def masked_softmax(s, mask):
    s = jnp.where(mask, s.astype(F32), -jnp.inf)
    m = jnp.max(s, axis=-1, keepdims=True)
    p = jnp.exp(s - jnp.where(jnp.isfinite(m), m, 0.0))
    den = jnp.sum(p, axis=-1, keepdims=True)
    return p / jnp.where(den > 0, den, 1.0)


def over_query_blocks(fn, block, qpos, *q_arrays):
    T = qpos.shape[0]
    if T <= block or T % block:
        return fn(qpos, *q_arrays)
    nb = T // block
    split = lambda a: jnp.moveaxis(a.reshape(a.shape[0], nb, block, *a.shape[2:]), 1, 0)
    out = lax.map(lambda args: fn(*args), (qpos.reshape(nb, block),) + tuple(split(a) for a in q_arrays))
    out = jnp.moveaxis(out, 0, 1)
    return out.reshape(out.shape[0], T, *out.shape[3:])


def gather_pages(pool, page_table):
    rows = pool[page_table]
    return rows.reshape(rows.shape[0], rows.shape[1] * rows.shape[2], *rows.shape[3:])


def split_cols(a, widths):
    return jnp.split(a, np.cumsum(widths)[:-1].tolist(), axis=-1)


def kv_rows(a, pos, n_heads, rotary):
    B, T = a.shape[:2]
    a = a.reshape(B, T, 2, n_heads, HEAD_DIM)
    k = rope(a[:, :, 0], pos) if rotary else a[:, :, 0]
    return jnp.stack([k, a[:, :, 1]], axis=2)


def cat_past(past, new):
    return new if past is None else jnp.concatenate([past, new], axis=1)


def nsa_mixer(h, start, w_in, w_out, past_cmp, past_slc, past_win):
    B, T, _ = h.shape
    R = N_HEADS // N_KV
    scale = HEAD_DIM ** -0.5
    pos = start + jnp.arange(T, dtype=jnp.int32)
    q, kvc, kvs, kvw, gl, z = split_cols(h @ w_in, [HD, KVD, KVD, KVD, 3 * N_HEADS, HD])
    q = rope(q.reshape(B, T, N_HEADS, HEAD_DIM), pos)
    kvc, kvs, kvw = (kv_rows(a, pos, N_KV, True) for a in (kvc, kvs, kvw))
    gates = jax.nn.sigmoid(gl.astype(F32)).reshape(B, T, N_HEADS, 3)
    keys_c, keys_s, keys_w = cat_past(past_cmp, kvc), cat_past(past_slc, kvs), cat_past(past_win, kvw)
    L = keys_c.shape[1]
    n_cmp = L // NSA_BLOCK
    cmp_kv = jnp.mean(keys_c[:, :n_cmp * NSA_BLOCK].astype(F32).reshape(B, n_cmp, NSA_BLOCK, 2, N_KV, HEAD_DIM), axis=2).astype(h.dtype)
    n_sel = -(-L // NSA_BLOCK)
    top_n = min(NSA_TOP_N, n_sel)
    sel_blocks = jnp.pad(keys_s, ((0, 0), (0, n_sel * NSA_BLOCK - L), (0, 0), (0, 0), (0, 0)))
    sel_blocks = jnp.moveaxis(sel_blocks.reshape(B, n_sel, NSA_BLOCK, 2, N_KV, HEAD_DIM), 4, 1)
    w_pos0 = start + T - keys_w.shape[1]
    keys_w_pad = jnp.pad(keys_w, ((0, 0), (NSA_WINDOW, 0), (0, 0), (0, 0), (0, 0)))
    cmp_blk = jnp.arange(n_cmp)
    sel_blk = jnp.arange(n_sel)
    blk_off = jnp.arange(NSA_BLOCK)
    b_idx = jnp.arange(B)[:, None, None, None]
    g_idx = jnp.arange(N_KV)[None, :, None, None]

    def block(qp, qb, gb):
        Tb = qp.shape[0]
        qg = qb.reshape(B, Tb, N_KV, R, HEAD_DIM)
        s = jnp.einsum('btgrd,bngd->bgrtn', qg, cmp_kv[:, :, 0], preferred_element_type=F32) * scale
        p = masked_softmax(s, (cmp_blk[None, :] + 1) * NSA_BLOCK - 1 <= qp[:, None])
        o_cmp = jnp.einsum('bgrtn,bngd->btgrd', p.astype(cmp_kv.dtype), cmp_kv[:, :, 1])
        imp = jnp.pad(jnp.sum(p, axis=2), ((0, 0), (0, 0), (0, 0), (0, n_sel - n_cmp)))
        cur = (qp // NSA_BLOCK)[:, None]
        score = jnp.where((sel_blk[None, :] == cur) | (sel_blk[None, :] == 0), jnp.inf,
                          jnp.where(sel_blk[None, :] < cur, imp, -jnp.inf))
        top_val, top_idx = lax.top_k(score, top_n)
        kv_sel = sel_blocks[b_idx, g_idx, top_idx]
        kv_sel = kv_sel.reshape(B, N_KV, Tb, top_n * NSA_BLOCK, 2, HEAD_DIM)
        kpos = top_idx[..., None] * NSA_BLOCK + blk_off
        ok = (top_val > -jnp.inf)[..., None] & (kpos <= qp[None, None, :, None, None])
        ok = ok.reshape(B, N_KV, Tb, top_n * NSA_BLOCK)
        s = jnp.einsum('btgrd,bgtsd->bgrts', qg, kv_sel[..., 0, :], preferred_element_type=F32) * scale
        p = masked_softmax(s, ok[:, :, None])
        o_sel = jnp.einsum('bgrts,bgtsd->btgrd', p.astype(kv_sel.dtype), kv_sel[..., 1, :])
        span = NSA_WINDOW + Tb - 1
        st = qp[0] - w_pos0 + 1
        kv_w = lax.dynamic_slice_in_dim(keys_w_pad, st, span, axis=1)
        kpos_w = (w_pos0 - NSA_WINDOW + st + jnp.arange(span))[None, :]
        okw = (kpos_w <= qp[:, None]) & (kpos_w > qp[:, None] - NSA_WINDOW) & (kpos_w >= 0)
        s = jnp.einsum('btgrd,bsgd->bgrts', qg, kv_w[:, :, 0], preferred_element_type=F32) * scale
        p = masked_softmax(s, okw)
        o_win = jnp.einsum('bgrts,bsgd->btgrd', p.astype(kv_w.dtype), kv_w[:, :, 1])
        g = gb.reshape(B, Tb, N_KV, R, 3)
        o = (g[..., 0:1] * o_cmp.astype(F32) + g[..., 1:2] * o_sel.astype(F32) + g[..., 2:3] * o_win.astype(F32))
        return o.astype(qb.dtype).reshape(B, Tb, HD)

    o = over_query_blocks(block, NSA_Q_BLOCK, pos, q, gates)
    y = (o * jax.nn.silu(z)) @ w_out
    new_win = keys_w[:, -min(NSA_WINDOW, keys_w.shape[1]):]
    return y, kvc, kvs, new_win


def fox_mixer(h, start, w_in, b_f, w_out, past_kv, past_logf):
    B, T, _ = h.shape
    scale = HEAD_DIM ** -0.5
    pos = start + jnp.arange(T, dtype=jnp.int32)
    q, k, v, fl, z = split_cols(h @ w_in, [HD, HD, HD, N_HEADS, HD])
    q = q.reshape(B, T, N_HEADS, HEAD_DIM)
    kv = kv_rows(jnp.concatenate([k, v], axis=-1), pos, N_HEADS, False)
    logf = jax.nn.log_sigmoid(fl.astype(F32) + b_f.astype(F32)).astype(h.dtype)
    keys = cat_past(past_kv, kv)
    lf = cat_past(past_logf, logf)
    L = keys.shape[1]
    csum = jnp.cumsum(lf.astype(F32), axis=1)
    c_key = jnp.moveaxis(csum, 1, 2)
    c_q = csum[:, L - T:]
    kpos = jnp.arange(L)

    def block(qp, qb, cb):
        Tb = qp.shape[0]
        s = jnp.einsum('bthd,bshd->bhts', qb, keys[:, :, 0], preferred_element_type=F32) * scale
        s = s + jnp.moveaxis(cb, 1, 2)[..., None] - c_key[:, :, None, :]
        p = masked_softmax(s, kpos[None, :] <= qp[:, None])
        o = jnp.einsum('bhts,bshd->bthd', p.astype(keys.dtype), keys[:, :, 1])
        return o.reshape(B, Tb, HD)

    o = over_query_blocks(block, Q_BLOCK, pos, q, c_q)
    y = (o * jax.nn.silu(z)) @ w_out
    return y, kv, logf


def dsa_mixer(h, start, w_in, w_out, past_kv, past_idx_k):
    B, T, _ = h.shape
    R = N_HEADS // N_KV
    scale = HEAD_DIM ** -0.5
    pos = start + jnp.arange(T, dtype=jnp.int32)
    q, kv, qi, ki, wi, z = split_cols(h @ w_in, [HD, KVD, IDX_HEADS * IDX_DIM, IDX_DIM, IDX_HEADS, HD])
    q = rope(q.reshape(B, T, N_HEADS, HEAD_DIM), pos)
    kv = kv_rows(kv, pos, N_KV, True)
    qi = rope(qi.reshape(B, T, IDX_HEADS, IDX_DIM), pos)
    ki = rope(ki.reshape(B, T, 1, IDX_DIM), pos)[:, :, 0]
    wi = wi.astype(F32) * IDX_HEADS ** -0.5
    keys = cat_past(past_kv, kv)
    idx_keys = cat_past(past_idx_k, ki)
    L = keys.shape[1]
    top_k = min(DSA_TOP_K, L // 4)
    kpos = jnp.arange(L)
    b_idx = jnp.arange(B)[:, None, None]

    def block(qp, qb, qib, wib):
        Tb = qp.shape[0]
        logits = jnp.einsum('bthe,bse->bths', qib, idx_keys, preferred_element_type=F32) * IDX_DIM ** -0.5
        score = jnp.einsum('bths,bth->bts', jax.nn.relu(logits), wib)
        score = jnp.where(kpos[None, None, :] <= qp[None, :, None], score, -jnp.inf)
        top_val, top_idx = lax.top_k(score, top_k)
        kv_sel = keys[b_idx, top_idx]
        qg = qb.reshape(B, Tb, N_KV, R, HEAD_DIM)
        s = jnp.einsum('btgrd,btsgd->bgrts', qg, kv_sel[:, :, :, 0], preferred_element_type=F32) * scale
        p = masked_softmax(s, (top_val > -jnp.inf)[:, None, None])
        o = jnp.einsum('bgrts,btsgd->btgrd', p.astype(kv_sel.dtype), kv_sel[:, :, :, 1])
        return o.reshape(B, Tb, HD)

    o = over_query_blocks(block, Q_BLOCK, pos, q, qi, wi)
    y = (o * jax.nn.silu(z)) @ w_out
    return y, kv, ki


def setup_inputs(seed: int = 0) -> dict:
    key = jax.random.key(seed)
    ks = jax.random.split(key, 24)
    n_pages = PAST_LEN // PAGE_SIZE
    n_used = DEC_BATCH * n_pages
    n_pool = n_used + max(1, n_used // 4)
    wbuf = min(NSA_WINDOW, PAST_LEN)
    nrm = lambda k, shape, s=1.0: s * jax.random.normal(k, shape, F32)
    page_table = jax.random.permutation(ks[2], n_pool)[:n_used].reshape(DEC_BATCH, n_pages).astype(jnp.int32)
    return {
        'x_prompt': nrm(ks[0], (BATCH, SEQ, D_MODEL)),
        'x_sample': nrm(ks[1], (DEC_BATCH, DEC_SEQ, D_MODEL)),
        'cache_nsa_cmp_kv': nrm(ks[3], (N_NSA, n_pool, PAGE_SIZE, 2, N_KV, HEAD_DIM)),
        'cache_nsa_slc_kv': nrm(ks[4], (N_NSA, n_pool, PAGE_SIZE, 2, N_KV, HEAD_DIM)),
        'state_nsa_win_kv': nrm(ks[5], (N_NSA, DEC_BATCH, wbuf, 2, N_KV, HEAD_DIM)),
        'cache_fox_kv': nrm(ks[6], (N_FOX, n_pool, PAGE_SIZE, 2, N_HEADS, HEAD_DIM)),
        'cache_fox_logf': jax.nn.log_sigmoid(FORGET_BIAS + nrm(ks[7], (N_FOX, n_pool, PAGE_SIZE, N_HEADS), 0.5)),
        'cache_dsa_kv': nrm(ks[8], (N_DSA, n_pool, PAGE_SIZE, 2, N_KV, HEAD_DIM)),
        'cache_dsa_idx_k': nrm(ks[9], (N_DSA, n_pool, PAGE_SIZE, IDX_DIM)),
        'page_table': page_table,
        'norm_g': 1.0 + nrm(ks[10], (DEPTH, D_MODEL), 0.02),
        'final_norm_g': 1.0 + nrm(ks[11], (D_MODEL,), 0.02),
        'nsa_w_in': nrm(ks[12], (N_NSA, D_MODEL, NSA_PROJ), D_MODEL ** -0.5),
        'nsa_w_out': nrm(ks[13], (N_NSA, HD, D_MODEL), HD ** -0.5),
        'fox_w_in': nrm(ks[14], (N_FOX, D_MODEL, FOX_PROJ), D_MODEL ** -0.5),
        'fox_b_f': FORGET_BIAS + nrm(ks[15], (N_FOX, N_HEADS), 0.1),
        'fox_w_out': nrm(ks[16], (N_FOX, HD, D_MODEL), HD ** -0.5),
        'dsa_w_in': nrm(ks[17], (N_DSA, D_MODEL, DSA_PROJ), D_MODEL ** -0.5),
        'dsa_w_out': nrm(ks[18], (N_DSA, HD, D_MODEL), HD ** -0.5),
    }


def reference(x_prompt, x_sample, cache_nsa_cmp_kv, cache_nsa_slc_kv, state_nsa_win_kv, cache_fox_kv,
              cache_fox_logf, cache_dsa_kv, cache_dsa_idx_k, page_table, norm_g, final_norm_g,
              nsa_w_in, nsa_w_out, fox_w_in, fox_b_f, fox_w_out, dsa_w_in, dsa_w_out):
    past_len = page_table.shape[1] * PAGE_SIZE
    xp, xs = x_prompt, x_sample
    nsa_c_p, nsa_c_s, nsa_s_p, nsa_s_s, nsa_w_p, nsa_w_s = [], [], [], [], [], []
    fox_kv_p, fox_kv_s, fox_lf_p, fox_lf_s = [], [], [], []
    dsa_kv_p, dsa_kv_s, dsa_ik_p, dsa_ik_s = [], [], [], []
    for i in range(DEPTH):
        kind, j = i % N_MIXERS, i // N_MIXERS
        hp = rms_norm(xp, norm_g[i])
        hs = rms_norm(xs, norm_g[i])
        if kind == 0:
            yp, c, s, w = nsa_mixer(hp, 0, nsa_w_in[j], nsa_w_out[j], None, None, None)
            nsa_c_p.append(c); nsa_s_p.append(s); nsa_w_p.append(w)
            ys, c, s, w = nsa_mixer(hs, past_len, nsa_w_in[j], nsa_w_out[j],
                                    gather_pages(cache_nsa_cmp_kv[j], page_table),
                                    gather_pages(cache_nsa_slc_kv[j], page_table),
                                    state_nsa_win_kv[j])
            nsa_c_s.append(c); nsa_s_s.append(s); nsa_w_s.append(w)
        elif kind == 1:
            yp, kv, lf = fox_mixer(hp, 0, fox_w_in[j], fox_b_f[j], fox_w_out[j], None, None)
            fox_kv_p.append(kv); fox_lf_p.append(lf)
            ys, kv, lf = fox_mixer(hs, past_len, fox_w_in[j], fox_b_f[j], fox_w_out[j],
                                   gather_pages(cache_fox_kv[j], page_table),
                                   gather_pages(cache_fox_logf[j], page_table))
            fox_kv_s.append(kv); fox_lf_s.append(lf)
        else:
            yp, kv, ik = dsa_mixer(hp, 0, dsa_w_in[j], dsa_w_out[j], None, None)
            dsa_kv_p.append(kv); dsa_ik_p.append(ik)
            ys, kv, ik = dsa_mixer(hs, past_len, dsa_w_in[j], dsa_w_out[j],
                                   gather_pages(cache_dsa_kv[j], page_table),
                                   gather_pages(cache_dsa_idx_k[j], page_table))
            dsa_kv_s.append(kv); dsa_ik_s.append(ik)
        xp = xp + yp
        xs = xs + ys
    y_prompt = rms_norm(xp, final_norm_g)
    y_sample = rms_norm(xs, final_norm_g)
    return (y_prompt, y_sample,
            jnp.stack(nsa_c_p), jnp.stack(nsa_c_s), jnp.stack(nsa_s_p), jnp.stack(nsa_s_s),
            jnp.stack(nsa_w_p), jnp.stack(nsa_w_s),
            jnp.stack(fox_kv_p), jnp.stack(fox_kv_s), jnp.stack(fox_lf_p), jnp.stack(fox_lf_s),
            jnp.stack(dsa_kv_p), jnp.stack(dsa_kv_s), jnp.stack(dsa_ik_p), jnp.stack(dsa_ik_s))
```

```python
import functools

import numpy as np
import jax
import jax.numpy as jnp
from jax import lax
from jax.experimental import pallas as pl
from jax.experimental.pallas import tpu as pltpu

F32 = jnp.float32
BF16 = jnp.bfloat16
NEG_INF = float("-inf")
POS_INF = float("inf")

HEAD_DIM = 64
N_KV = 4
ROPE_THETA = 10000.0
RMS_EPS = 1e-6
NSA_BLOCK = 64
NSA_TOP_N = 16
NSA_WINDOW = 512
IDX_HEADS = 8
IDX_DIM = 64
DSA_TOP_K = 256
N_MIXERS = 3

LANES = 128
SUBLANES_BF16 = 16
VMEM_LIMIT = 56 * 1024 * 1024

ROWS_PAD = SUBLANES_BF16
M_INIT = -1e30


def _cparams(n_axes):
    return pltpu.CompilerParams(dimension_semantics=("arbitrary",) * n_axes,
                                vmem_limit_bytes=VMEM_LIMIT)


def _dot(a, b):
    return jnp.dot(a, b, preferred_element_type=F32)


def _dot_nt(a, b):
    return lax.dot_general(a, b, (((1,), (1,)), ((), ())), preferred_element_type=F32)


def _iota(shape, dim):
    return lax.broadcasted_iota(jnp.int32, shape, dim)


def _log2(n):
    assert n & (n - 1) == 0
    return n.bit_length() - 1


def _softmax_parts(s):
    m = jnp.max(s, axis=-1, keepdims=True)
    m = jnp.where(m > NEG_INF, m, 0.0)
    p = jnp.exp(s - m)
    return p, jnp.sum(p, axis=-1, keepdims=True)


def _safe_den(den):
    return jnp.where(den > 0, den, 1.0)


def _online_update(s, vv, m_i, l_i, acc):
    m_new = jnp.maximum(m_i, jnp.max(s, axis=-1, keepdims=True))
    alpha = jnp.exp(m_i - m_new)
    p = jnp.exp(s - m_new)
    l_new = alpha * l_i + jnp.sum(p, axis=-1, keepdims=True)
    acc_new = alpha * acc + _dot(p.astype(BF16), vv)
    return m_new, l_new, acc_new


def _sigmoid(x):
    return 1.0 / (1.0 + jnp.exp(-x))


class _Seg:
    def __init__(self, col, width, rope=0, logsig=False, outs=((F32, 1.0),)):
        self.col, self.width, self.rope, self.logsig, self.outs = col, width, rope, logsig, outs


def _rope(y, cos, sin):
    w = y.shape[1]
    reps = w // LANES
    cw = jnp.concatenate([cos] * reps, axis=1) if reps > 1 else cos
    sw = jnp.concatenate([sin] * reps, axis=1) if reps > 1 else sin
    half = HEAD_DIM // 2
    first = (_iota(y.shape, 1) & (HEAD_DIM - 1)) < half
    rot = jnp.where(first, pltpu.roll(y, w - half, 1), pltpu.roll(y, half, 1))
    return y * cw + rot * sw


def _log_sigmoid(x):
    return -(jnp.maximum(-x, 0.0) + jnp.log1p(jnp.exp(-jnp.abs(x))))


def _proj_kernel(x_ref, g_ref, w_ref, cos_ref, sin_ref, b_ref, *out_refs, segs):
    x = x_ref[...]
    h = x * lax.rsqrt(jnp.mean(x * x, axis=-1, keepdims=True) + RMS_EPS)
    hb = (h * g_ref[...]).astype(BF16)
    oi = 0
    for seg in segs:
        y = _dot(hb, w_ref[:, seg.col:seg.col + seg.width])
        if seg.rope == seg.width:
            y = _rope(y, cos_ref[...], sin_ref[...])
        elif seg.rope:
            y = jnp.concatenate([_rope(y[:, :seg.rope], cos_ref[...], sin_ref[...]), y[:, seg.rope:]], axis=1)
        if seg.logsig:
            y = _log_sigmoid(y + b_ref[...])
        for dtype, scale in seg.outs:
            out_refs[oi][...] = (y if scale == 1.0 else y * scale).astype(dtype)
            oi += 1


def _project(x, g, w_bf16, bias, cos, sin, segs, tm):
    n, d = x.shape
    assert n % tm == 0 and cos.shape[0] % tm == 0
    n_pos_tiles = cos.shape[0] // tm
    out_shape, out_specs = [], []
    for seg in segs:
        for dtype, _ in seg.outs:
            out_shape.append(jax.ShapeDtypeStruct((n, seg.width), dtype))
            out_specs.append(pl.BlockSpec((tm, seg.width), lambda i: (i, 0)))
    return pl.pallas_call(
        functools.partial(_proj_kernel, segs=segs),
        out_shape=out_shape,
        grid=(n // tm,),
        in_specs=[
            pl.BlockSpec((tm, d), lambda i: (i, 0)),
            pl.BlockSpec((1, d), lambda i: (0, 0)),
            pl.BlockSpec(w_bf16.shape, lambda i: (0, 0)),
            pl.BlockSpec((tm, LANES), lambda i: (i % n_pos_tiles, 0)),
            pl.BlockSpec((tm, LANES), lambda i: (i % n_pos_tiles, 0)),
            pl.BlockSpec((1, LANES), lambda i: (0, 0)),
        ],
        out_specs=out_specs,
        compiler_params=_cparams(1),
        name="rmsnorm_proj",
    )(x, g.reshape(1, d), w_bf16, cos, sin, bias)


def _out_kernel(o_ref, z_ref, x_ref, w_ref, g_ref, y_ref, *yn_ref):
    z = z_ref[...]
    a = o_ref[...] * (z * _sigmoid(z))
    y = x_ref[...] + _dot(a.astype(BF16), w_ref[...])
    y_ref[...] = y
    if yn_ref:
        yn_ref[0][...] = y * lax.rsqrt(jnp.mean(y * y, axis=-1, keepdims=True) + RMS_EPS) * g_ref[...]


def _out_project(o, z, x, w_bf16, final_g, tm):
    n, d = x.shape
    hd = o.shape[1]
    row = lambda w: pl.BlockSpec((tm, w), lambda i: (i, 0))
    final = final_g is not None
    g = final_g.reshape(1, d) if final else jnp.zeros((1, d), F32)
    out_shape = [jax.ShapeDtypeStruct((n, d), F32)] * (2 if final else 1)
    res = pl.pallas_call(
        _out_kernel,
        out_shape=out_shape,
        grid=(n // tm,),
        in_specs=[row(hd), row(hd), row(d), pl.BlockSpec((hd, d), lambda i: (0, 0)),
                  pl.BlockSpec((1, d), lambda i: (0, 0))],
        out_specs=[row(d)] * (2 if final else 1),
        compiler_params=_cparams(1),
        name="gated_out_proj",
    )(o, z, x, w_bf16, g)
    return res


def _block_means(rows, n_blocks):
    w = rows.shape[1]
    means = jnp.sum(rows.reshape(n_blocks, NSA_BLOCK, w), axis=1) * (1.0 / NSA_BLOCK)
    return jnp.concatenate([means, jnp.zeros((LANES - n_blocks, w), F32)], axis=0)


def _blockmean_kernel(x_ref, o_ref, *, n_blocks):
    o_ref[0] = _block_means(x_ref[...], n_blocks)


def _blockmean_dense(kvc, batch, seq):
    w = kvc.shape[1]
    n_blocks = seq // NSA_BLOCK
    assert n_blocks % 8 == 0 and n_blocks <= LANES
    return pl.pallas_call(
        functools.partial(_blockmean_kernel, n_blocks=n_blocks),
        out_shape=jax.ShapeDtypeStruct((batch, LANES, w), F32),
        grid=(batch,),
        in_specs=[pl.BlockSpec((seq, w), lambda b: (b, 0))],
        out_specs=pl.BlockSpec((1, LANES, w), lambda b: (b, 0, 0)),
        compiler_params=_cparams(1),
        name="nsa_blockmean",
    )(kvc)


def _blockmean_paged_kernel(pt_ref, *refs, n_pages, page, n_blocks):
    page_refs, o_ref, buf = refs[:n_pages], refs[n_pages], refs[n_pages + 1]
    for p in range(n_pages):
        buf[p * page:(p + 1) * page, :] = page_refs[p][0]
    o_ref[0] = _block_means(buf[0:n_blocks * NSA_BLOCK, :], n_blocks)


def _page_specs(n_pages, page, w):
    return [pl.BlockSpec((1, page, w), lambda b, pt, p=p: (pt[b, p], 0, 0)) for p in range(n_pages)]


def _blockmean_paged(pool, page_table, n_blocks):
    _, page, w = pool.shape
    batch, n_pages = page_table.shape
    assert n_blocks * NSA_BLOCK <= n_pages * page and n_blocks % 8 == 0
    return pl.pallas_call(
        functools.partial(_blockmean_paged_kernel, n_pages=n_pages, page=page, n_blocks=n_blocks),
        out_shape=jax.ShapeDtypeStruct((batch, LANES, w), F32),
        grid_spec=pltpu.PrefetchScalarGridSpec(
            num_scalar_prefetch=1, grid=(batch,),
            in_specs=_page_specs(n_pages, page, w),
            out_specs=pl.BlockSpec((1, LANES, w), lambda b, pt: (b, 0, 0)),
            scratch_shapes=[pltpu.VMEM((n_pages * page, w), F32)]),
        compiler_params=_cparams(1),
        name="nsa_blockmean_paged",
    )(page_table, *([pool] * n_pages))


def _stack_heads(q, g, rep):
    return jnp.concatenate(
        [q[:, (g * rep + r) * HEAD_DIM:(g * rep + r + 1) * HEAD_DIM] for r in range(rep)], axis=0)


def _rep_rows(a, rep):
    return jnp.concatenate([a] * rep, axis=0)


def _nsa_compressed(qg, cmp_bf, g, t, n_cmp, rows, rep, kv_width):
    half = kv_width // 2
    ck = cmp_bf[:, g * HEAD_DIM:(g + 1) * HEAD_DIM]
    cv = cmp_bf[:, half + g * HEAD_DIM:half + (g + 1) * HEAD_DIM]
    blk = _iota((1, LANES), 1)
    visible = (blk < n_cmp) & ((blk + 1) * NSA_BLOCK - 1 <= t)
    s = jnp.where(_rep_rows(visible, rep), _dot_nt(qg, ck), NEG_INF)
    p, den = _softmax_parts(s)
    p = p / _safe_den(den)
    o_cmp = _dot(p.astype(BF16), cv)
    imp = p[0:rows]
    for r in range(1, rep):
        imp = imp + p[r * rows:(r + 1) * rows]
    return o_cmp, imp


def _nsa_select(imp, t, n_sel):
    blk = _iota((1, LANES), 1)
    cur = t >> _log2(NSA_BLOCK)
    score = jnp.where((blk == cur) | (blk == 0), POS_INF, jnp.where(blk < cur, imp, NEG_INF))
    score = jnp.where(blk < n_sel, score, NEG_INF)
    rank = jnp.zeros(score.shape, F32)
    for m in range(n_sel):
        sm = score[:, m:m + 1]
        beats = (sm > score) | ((sm == score) & (blk > m))
        rank = rank + jnp.where(beats, 1.0, 0.0)
    top_n = min(NSA_TOP_N, n_sel)
    return jnp.where((rank < top_n) & (score > NEG_INF), 1.0, 0.0)


def _expand_blocks(sel, kpos):
    onehot = (kpos >> _log2(NSA_BLOCK)) == _iota((LANES, kpos.shape[1]), 0)
    return _dot(sel.astype(BF16), jnp.where(onehot, 1.0, 0.0).astype(BF16))


def _nsa_combine(gates, g, rep, rows, n_heads, o_cmp, o_sel, o_win):
    outs = []
    for r in range(rep):
        h = g * rep + r
        sl = slice(r * rows, (r + 1) * rows)
        outs.append(gates[:, h:h + 1] * o_cmp[sl]
                    + gates[:, n_heads + h:n_heads + h + 1] * o_sel[sl]
                    + gates[:, 2 * n_heads + h:2 * n_heads + h + 1] * o_win[sl])
    return outs


def _nsa_prompt_kernel(q_ref, gl_ref, cmp_ref, ks_ref, kw_ref, o_ref, *, tq, kc, seq, n_heads):
    rep = n_heads // N_KV
    kv_width = ks_ref.shape[1]
    half = kv_width // 2
    q0 = pl.program_id(1) * tq
    t = q0 + _iota((tq, 1), 0)
    n_blk = seq // NSA_BLOCK
    q = q_ref[...]
    gates = _sigmoid(gl_ref[...])
    cmp_bf = cmp_ref[0].astype(BF16)
    n_chunks = (q0 + tq + kc - 1) // kc
    win_lo = pl.multiple_of(jnp.maximum(q0 - NSA_WINDOW, 0), tq)
    win_len = NSA_WINDOW + tq
    outs = []
    for g in range(N_KV):
        qg = _stack_heads(q, g, rep)
        o_cmp, imp = _nsa_compressed(qg, cmp_bf, g, t, n_blk, tq, rep, kv_width)
        sel = _nsa_select(imp, t, n_blk)

        def sel_body(c, carry, qg=qg, sel=sel, g=g):
            k0 = pl.multiple_of(c * kc, kc)
            kk = ks_ref[pl.ds(k0, kc), g * HEAD_DIM:(g + 1) * HEAD_DIM]
            vv = ks_ref[pl.ds(k0, kc), half + g * HEAD_DIM:half + (g + 1) * HEAD_DIM]
            kpos = k0 + _iota((1, kc), 1)
            allowed = (_expand_blocks(sel, kpos) > 0.5) & (kpos <= t)
            s = jnp.where(_rep_rows(allowed, rep), _dot_nt(qg, kk), NEG_INF)
            return _online_update(s, vv, *carry)

        init = (jnp.full((rep * tq, 1), M_INIT, F32), jnp.zeros((rep * tq, 1), F32),
                jnp.zeros((rep * tq, HEAD_DIM), F32))
        _, l_i, acc = lax.fori_loop(0, n_chunks, sel_body, init)
        o_sel = acc / _safe_den(l_i)

        kk = kw_ref[pl.ds(win_lo, win_len), g * HEAD_DIM:(g + 1) * HEAD_DIM]
        vv = kw_ref[pl.ds(win_lo, win_len), half + g * HEAD_DIM:half + (g + 1) * HEAD_DIM]
        kpos = win_lo + _iota((1, win_len), 1)
        okw = (kpos <= t) & (kpos > t - NSA_WINDOW)
        s = jnp.where(_rep_rows(okw, rep), _dot_nt(qg, kk), NEG_INF)
        p, den = _softmax_parts(s)
        o_win = _dot(p.astype(BF16), vv) / _safe_den(den)
        outs += _nsa_combine(gates, g, rep, tq, n_heads, o_cmp, o_sel, o_win)
    o_ref[...] = jnp.concatenate(outs, axis=1)


def _nsa_prompt_attention(q_bf, gl, cmp_means, ks_bf, kw_bf, batch, seq, tq=128, kc=512):
    n, hd = q_bf.shape
    n_heads = hd // HEAD_DIM
    kvw = ks_bf.shape[1]
    assert seq % kc == 0 and seq % tq == 0 and NSA_WINDOW % tq == 0 and NSA_WINDOW + tq <= seq
    assert seq // NSA_BLOCK <= LANES and seq % NSA_BLOCK == 0
    nq = seq // tq
    return pl.pallas_call(
        functools.partial(_nsa_prompt_kernel, tq=tq, kc=kc, seq=seq, n_heads=n_heads),
        out_shape=jax.ShapeDtypeStruct((n, hd), F32),
        grid=(batch, nq),
        in_specs=[
            pl.BlockSpec((tq, hd), lambda b, i: (b * nq + i, 0)),
            pl.BlockSpec((tq, LANES), lambda b, i: (b * nq + i, 0)),
            pl.BlockSpec((1, LANES, kvw), lambda b, i: (b, 0, 0)),
            pl.BlockSpec((seq, kvw), lambda b, i: (b, 0)),
            pl.BlockSpec((seq, kvw), lambda b, i: (b, 0)),
        ],
        out_specs=pl.BlockSpec((tq, hd), lambda b, i: (b * nq + i, 0)),
        compiler_params=_cparams(2),
        name="nsa_prompt_attention",
    )(q_bf, gl, cmp_means, ks_bf, kw_bf)


def _pad_rows(new_rows, total):
    return jnp.concatenate([new_rows, jnp.zeros((total - new_rows.shape[0], new_rows.shape[1]), new_rows.dtype)], axis=0)


def _nsa_decode_kernel(pt_ref, q_ref, gl_ref, cmp_ref, ksn_ref, kwn_ref, win_ref, *refs,
                       n_pages, page, past, t_new, n_heads):
    page_refs = refs[:n_pages]
    o_ref, kbuf, wbuf = refs[n_pages:]
    rep = n_heads // N_KV
    kv_width = kbuf.shape[1]
    half = kv_width // 2
    rows = q_ref.shape[1]
    total = past + t_new
    n_cmp = total // NSA_BLOCK
    n_sel = -(-total // NSA_BLOCK)
    w_state = win_ref.shape[1]
    s_pad = kbuf.shape[0]
    for p in range(n_pages):
        kbuf[p * page:(p + 1) * page, :] = page_refs[p][0].astype(BF16)
    kbuf[past:s_pad, :] = _pad_rows(ksn_ref[0], s_pad - past)
    wbuf[0:w_state, :] = win_ref[0].astype(BF16)
    wbuf[w_state:, :] = _pad_rows(kwn_ref[0], wbuf.shape[0] - w_state)

    t = past + _iota((rows, 1), 0)
    q = q_ref[0]
    gates = _sigmoid(gl_ref[0])
    cmp_bf = cmp_ref[0].astype(BF16)
    kpos = _iota((1, s_pad), 1)
    kpos_w = (past - w_state) + _iota((1, wbuf.shape[0]), 1)
    okw = (kpos_w <= t) & (kpos_w > t - NSA_WINDOW) & (kpos_w >= 0)
    outs = []
    for g in range(N_KV):
        qg = _stack_heads(q, g, rep)
        o_cmp, imp = _nsa_compressed(qg, cmp_bf, g, t, n_cmp, rows, rep, kv_width)
        sel = _nsa_select(imp, t, n_sel)
        allowed = (_expand_blocks(sel, kpos) > 0.5) & (kpos <= t)
        s = jnp.where(_rep_rows(allowed, rep), _dot_nt(qg, kbuf[:, g * HEAD_DIM:(g + 1) * HEAD_DIM]), NEG_INF)
        p, den = _softmax_parts(s)
        o_sel = _dot(p.astype(BF16), kbuf[:, half + g * HEAD_DIM:half + (g + 1) * HEAD_DIM]) / _safe_den(den)
        s = jnp.where(_rep_rows(okw, rep), _dot_nt(qg, wbuf[:, g * HEAD_DIM:(g + 1) * HEAD_DIM]), NEG_INF)
        p, den = _softmax_parts(s)
        o_win = _dot(p.astype(BF16), wbuf[:, half + g * HEAD_DIM:half + (g + 1) * HEAD_DIM]) / _safe_den(den)
        outs += _nsa_combine(gates, g, rep, rows, n_heads, o_cmp, o_sel, o_win)
    o_ref[0] = jnp.concatenate(outs, axis=1)


def _nsa_decode_attention(q_bf, gl, cmp_means, ks_new, kw_new, win_state, slc_pool, page_table, t_new):
    batch, rows, hd = q_bf.shape
    n_heads = hd // HEAD_DIM
    _, page, kvw = slc_pool.shape
    n_pages = page_table.shape[1]
    past = n_pages * page
    w_state = win_state.shape[1]
    assert past % NSA_BLOCK == 0 and -(-(past + t_new) // NSA_BLOCK) <= LANES
    per_b = lambda *shape: pl.BlockSpec((1,) + shape, lambda b, pt: (b,) + (0,) * len(shape))
    return pl.pallas_call(
        functools.partial(_nsa_decode_kernel, n_pages=n_pages, page=page, past=past, t_new=t_new,
                          n_heads=n_heads),
        out_shape=jax.ShapeDtypeStruct((batch, rows, hd), F32),
        grid_spec=pltpu.PrefetchScalarGridSpec(
            num_scalar_prefetch=1, grid=(batch,),
            in_specs=[per_b(rows, hd), per_b(rows, LANES), per_b(LANES, kvw), per_b(rows, kvw),
                      per_b(rows, kvw), per_b(w_state, kvw)] + _page_specs(n_pages, page, kvw),
            out_specs=per_b(rows, hd),
            scratch_shapes=[pltpu.VMEM((past + page, kvw), BF16), pltpu.VMEM((w_state + page, kvw), BF16)]),
        compiler_params=_cparams(1),
        name="nsa_decode_attention",
    )(page_table, q_bf, gl, cmp_means, ks_new, kw_new, win_state, *([slc_pool] * n_pages))


def _cumsum_rows(x, tri_bf):
    hi = x.astype(BF16)
    r1 = x - hi.astype(F32)
    mid = r1.astype(BF16)
    lo = (r1 - mid.astype(F32)).astype(BF16)
    return _dot(tri_bf, hi) + _dot(tri_bf, mid) + _dot(tri_bf, lo)


def _lower_tri(n):
    return jnp.where(_iota((n, n), 0) >= _iota((n, n), 1), 1.0, 0.0).astype(BF16)


def _cumsum_kernel(x_ref, o_ref, *, chunk):
    n = x_ref.shape[1]
    tri = _lower_tri(chunk)
    carry = jnp.zeros((1, x_ref.shape[2]), F32)
    for c in range(n // chunk):
        cs = _cumsum_rows(x_ref[0, c * chunk:(c + 1) * chunk, :], tri) + carry
        o_ref[0, c * chunk:(c + 1) * chunk, :] = cs
        carry = cs[chunk - 1:chunk, :]


def _cumsum_dense(lf, chunk=256):
    batch, seq, h = lf.shape
    assert seq % chunk == 0
    return pl.pallas_call(
        functools.partial(_cumsum_kernel, chunk=chunk),
        out_shape=jax.ShapeDtypeStruct(lf.shape, F32),
        grid=(batch,),
        in_specs=[pl.BlockSpec((1, seq, h), lambda b: (b, 0, 0))],
        out_specs=pl.BlockSpec((1, seq, h), lambda b: (b, 0, 0)),
        compiler_params=_cparams(1),
        name="fox_cumsum",
    )(lf)


def _cumsum_paged_kernel(pt_ref, new_ref, *refs, n_pages, page, n_heads):
    page_refs, o_ref = refs[:n_pages], refs[n_pages]
    tri = _lower_tri(page)
    carry = jnp.zeros((1, n_heads), F32)
    for p in range(n_pages):
        cs = _cumsum_rows(page_refs[p][0], tri) + carry
        o_ref[0, p * page:(p + 1) * page, :] = cs
        carry = cs[page - 1:page, :]
    new = _pad_rows(new_ref[0][:, :n_heads], page)
    o_ref[0, n_pages * page:(n_pages + 1) * page, :] = _cumsum_rows(new, tri) + carry


def _cumsum_paged(pool, page_table, lf_new):
    _, page, h = pool.shape
    batch, n_pages = page_table.shape
    rows = lf_new.shape[1]
    return pl.pallas_call(
        functools.partial(_cumsum_paged_kernel, n_pages=n_pages, page=page, n_heads=h),
        out_shape=jax.ShapeDtypeStruct((batch, (n_pages + 1) * page, h), F32),
        grid_spec=pltpu.PrefetchScalarGridSpec(
            num_scalar_prefetch=1, grid=(batch,),
            in_specs=[pl.BlockSpec((1, rows, LANES), lambda b, pt: (b, 0, 0))] + _page_specs(n_pages, page, h),
            out_specs=pl.BlockSpec((1, (n_pages + 1) * page, h), lambda b, pt: (b, 0, 0))),
        compiler_params=_cparams(1),
        name="fox_cumsum_paged",
    )(page_table, lf_new, *([pool] * n_pages))


HEADS_PER_BLOCK = LANES // HEAD_DIM


def _fox_prompt_kernel(q_ref, k_ref, v_ref, cq_ref, ck_ref, o_ref, *, tq, kc):
    q0 = pl.program_id(2) * tq
    t = q0 + _iota((tq, 1), 0)
    n_chunks = (q0 + tq + kc - 1) // kc
    outs = []
    for j in range(HEADS_PER_BLOCK):
        cols = slice(j * HEAD_DIM, (j + 1) * HEAD_DIM)
        qh = q_ref[:, cols]
        cq = cq_ref[0, 0][:, j:j + 1]

        def body(c, carry, qh=qh, cq=cq, cols=cols, j=j):
            k0 = pl.multiple_of(c * kc, kc)
            kk = k_ref[pl.ds(k0, kc), cols]
            vv = v_ref[pl.ds(k0, kc), cols]
            ck = ck_ref[0, 0, j:j + 1, pl.ds(k0, kc)]
            kpos = k0 + _iota((1, kc), 1)
            s = jnp.where(kpos <= t, _dot_nt(qh, kk) + (cq - ck), NEG_INF)
            return _online_update(s, vv, *carry)

        init = (jnp.full((tq, 1), M_INIT, F32), jnp.zeros((tq, 1), F32), jnp.zeros((tq, HEAD_DIM), F32))
        _, l_i, acc = lax.fori_loop(0, n_chunks, body, init)
        outs.append(acc / _safe_den(l_i))
    o_ref[...] = jnp.concatenate(outs, axis=1)


def _fox_prompt_attention(q_bf, kv_bf, csum, batch, seq, tq=256, kc=512):
    n, hd = q_bf.shape
    n_heads = hd // HEAD_DIM
    n_hb = hd // LANES
    nq = seq // tq
    assert seq % tq == 0 and seq % kc == 0
    c_pairs = csum.reshape(batch, seq, n_hb, HEADS_PER_BLOCK)
    cq = jnp.transpose(c_pairs, (0, 2, 1, 3))
    ck = jnp.transpose(c_pairs, (0, 2, 3, 1))
    return pl.pallas_call(
        functools.partial(_fox_prompt_kernel, tq=tq, kc=kc),
        out_shape=jax.ShapeDtypeStruct((n, hd), F32),
        grid=(batch, n_hb, nq),
        in_specs=[
            pl.BlockSpec((tq, LANES), lambda b, h, i: (b * nq + i, h)),
            pl.BlockSpec((seq, LANES), lambda b, h, i: (b, h)),
            pl.BlockSpec((seq, LANES), lambda b, h, i: (b, n_hb + h)),
            pl.BlockSpec((1, 1, tq, HEADS_PER_BLOCK), lambda b, h, i: (b, h, i, 0)),
            pl.BlockSpec((1, 1, HEADS_PER_BLOCK, seq), lambda b, h, i: (b, h, 0, 0)),
        ],
        out_specs=pl.BlockSpec((tq, LANES), lambda b, h, i: (b * nq + i, h)),
        compiler_params=_cparams(3),
        name="fox_prompt_attention",
    )(q_bf, kv_bf, kv_bf, cq, ck)


def _fox_decode_kernel(pt_ref, q_ref, kvn_ref, cq_ref, ck_ref, *refs, n_pages, page, past, n_heads):
    page_refs = refs[:n_pages]
    o_ref, kbuf = refs[n_pages:]
    s_pad = kbuf.shape[0]
    hd = n_heads * HEAD_DIM
    rows = q_ref.shape[1]
    for p in range(n_pages):
        kbuf[p * page:(p + 1) * page, :] = page_refs[p][0].astype(BF16)
    kbuf[past:s_pad, :] = _pad_rows(kvn_ref[0], s_pad - past)
    t = past + _iota((rows, 1), 0)
    visible = _iota((1, s_pad), 1) <= t
    q = q_ref[0]
    cq = cq_ref[0]
    outs = []
    for h in range(n_heads):
        cols = slice(h * HEAD_DIM, (h + 1) * HEAD_DIM)
        s = _dot_nt(q[:, cols], kbuf[:, cols]) + (cq[:, h:h + 1] - ck_ref[0, h:h + 1, :])
        p, den = _softmax_parts(jnp.where(visible, s, NEG_INF))
        outs.append(_dot(p.astype(BF16), kbuf[:, hd + h * HEAD_DIM:hd + (h + 1) * HEAD_DIM]) / _safe_den(den))
    o_ref[0] = jnp.concatenate(outs, axis=1)


def _fox_decode_attention(q_bf, kv_new_bf, c_all, kv_pool, page_table):
    batch, rows, hd = q_bf.shape
    n_heads = hd // HEAD_DIM
    _, page, kvw = kv_pool.shape
    n_pages = page_table.shape[1]
    past = n_pages * page
    s_pad = past + page
    assert past % rows == 0
    c_keys = jnp.transpose(c_all, (0, 2, 1))
    per_b = lambda *shape: pl.BlockSpec((1,) + shape, lambda b, pt: (b,) + (0,) * len(shape))
    return pl.pallas_call(
        functools.partial(_fox_decode_kernel, n_pages=n_pages, page=page, past=past, n_heads=n_heads),
        out_shape=jax.ShapeDtypeStruct((batch, rows, hd), F32),
        grid_spec=pltpu.PrefetchScalarGridSpec(
            num_scalar_prefetch=1, grid=(batch,),
            in_specs=[per_b(rows, hd), per_b(rows, kvw),
                      pl.BlockSpec((1, rows, n_heads), lambda b, pt: (b, past // rows, 0)),
                      per_b(n_heads, s_pad)] + _page_specs(n_pages, page, kvw),
            out_specs=per_b(rows, hd),
            scratch_shapes=[pltpu.VMEM((s_pad, kvw), BF16)]),
        compiler_params=_cparams(1),
        name="fox_decode_attention",
    )(page_table, q_bf, kv_new_bf, c_all, c_keys, *([kv_pool] * n_pages))


def _count(mask):
    return jnp.sum(jnp.where(mask, 1.0, 0.0), axis=-1, keepdims=True)


def _topk_bias(score, key_ref, top_k):
    rows, n = score.shape
    bits = lax.bitcast_convert_type(score + 0.0, jnp.int32)
    key_ref[...] = bits ^ ((bits >> 31) & jnp.int32(0x7FFFFFFF))
    int_min = jnp.int32(-2 ** 31)

    def value_bit(i, kth):
        cand = kth + (jnp.int32(1) << (31 - i))
        return jnp.where(_count(key_ref[...] >= cand) >= top_k, cand, kth)

    kth = lax.fori_loop(0, 32, value_bit, jnp.full((rows, 1), int_min, jnp.int32))
    key = key_ref[...]
    above = key > kth
    tied = key == kth
    need = top_k - _count(above)
    idx = _iota((1, n), 1)
    idx_bits = (n - 1).bit_length()

    def index_bit(i, last):
        cand = last + (jnp.int32(1) << (idx_bits - 1 - i))
        return jnp.where(_count(tied & (idx < cand)) < need, cand, last)

    last = lax.fori_loop(0, idx_bits, index_bit, jnp.zeros((rows, 1), jnp.int32))
    chosen = (above | (tied & (idx <= last))) & (score > NEG_INF)
    return jnp.where(chosen, 0.0, NEG_INF)


def _indexer_scores(qi, wi, ki, t, kpos):
    rows = qi.shape[0]
    qs = jnp.concatenate([qi[:, e * IDX_DIM:(e + 1) * IDX_DIM] for e in range(IDX_HEADS)], axis=0)
    logits = jnp.maximum(_dot_nt(qs, ki), 0.0)
    w = wi * (IDX_HEADS ** -0.5)
    score = logits[0:rows] * w[:, 0:1]
    for e in range(1, IDX_HEADS):
        score = score + logits[e * rows:(e + 1) * rows] * w[:, e:e + 1]
    return jnp.where(kpos <= t, score, NEG_INF)


def _dsa_prompt_kernel(q_ref, qi_ref, wi_ref, ki_ref, kv_ref, o_ref, score_ref, key_ref, *, tq, kc, seq, n_heads):
    rep = n_heads // N_KV
    half = kv_ref.shape[1] // 2
    q0 = pl.program_id(1) * tq
    t = q0 + _iota((tq, 1), 0)
    n_chunks = (q0 + tq + kc - 1) // kc
    top_k = min(DSA_TOP_K, seq // 4)
    qi = qi_ref[...]
    wi = wi_ref[...]
    for c in range(seq // kc):
        kpos = c * kc + _iota((1, kc), 1)

        @pl.when(c < n_chunks)
        def _(c=c, kpos=kpos):
            score_ref[:, c * kc:(c + 1) * kc] = _indexer_scores(
                qi, wi, ki_ref[c * kc:(c + 1) * kc, 0:IDX_DIM], t, kpos)

        @pl.when(c >= n_chunks)
        def _(c=c):
            score_ref[:, c * kc:(c + 1) * kc] = jnp.full((tq, kc), NEG_INF, F32)

    score_ref[...] = _topk_bias(score_ref[...], key_ref, top_k)
    q = q_ref[...]
    outs = []
    for g in range(N_KV):
        qg = _stack_heads(q, g, rep)

        def body(c, carry, qg=qg, g=g):
            k0 = pl.multiple_of(c * kc, kc)
            kk = kv_ref[pl.ds(k0, kc), g * HEAD_DIM:(g + 1) * HEAD_DIM]
            vv = kv_ref[pl.ds(k0, kc), half + g * HEAD_DIM:half + (g + 1) * HEAD_DIM]
            s = _dot_nt(qg, kk) + _rep_rows(score_ref[:, pl.ds(k0, kc)], rep)
            return _online_update(s, vv, *carry)

        init = (jnp.full((rep * tq, 1), M_INIT, F32), jnp.zeros((rep * tq, 1), F32),
                jnp.zeros((rep * tq, HEAD_DIM), F32))
        _, l_i, acc = lax.fori_loop(0, n_chunks, body, init)
        o_g = acc / _safe_den(l_i)
        outs += [o_g[r * tq:(r + 1) * tq] for r in range(rep)]
    o_ref[...] = jnp.concatenate(outs, axis=1)


def _dsa_prompt_attention(q_bf, qi_bf, wi, ki_bf, kv_bf, batch, seq, tq=128, kc=512):
    n, hd = q_bf.shape
    n_heads = hd // HEAD_DIM
    kvw = kv_bf.shape[1]
    nq = seq // tq
    assert seq % tq == 0 and seq % kc == 0
    row = lambda w: pl.BlockSpec((tq, w), lambda b, i: (b * nq + i, 0))
    return pl.pallas_call(
        functools.partial(_dsa_prompt_kernel, tq=tq, kc=kc, seq=seq, n_heads=n_heads),
        out_shape=jax.ShapeDtypeStruct((n, hd), F32),
        grid=(batch, nq),
        in_specs=[row(hd), row(qi_bf.shape[1]), row(LANES),
                  pl.BlockSpec((seq, LANES), lambda b, i: (b, 0)),
                  pl.BlockSpec((seq, kvw), lambda b, i: (b, 0))],
        out_specs=row(hd),
        scratch_shapes=[pltpu.VMEM((tq, seq), F32), pltpu.VMEM((tq, seq), jnp.int32)],
        compiler_params=_cparams(2),
        name="dsa_prompt_attention",
    )(q_bf, qi_bf, wi, ki_bf, kv_bf)


def _dsa_decode_score_kernel(pt_ref, qi_ref, wi_ref, kin_ref, *refs, n_pages, page, past):
    page_refs = refs[:n_pages]
    o_ref, kbuf = refs[n_pages:]
    s_pad = kbuf.shape[0]
    rows = qi_ref.shape[1]
    for p in range(n_pages):
        kbuf[p * page:(p + 1) * page, :] = page_refs[p][0].astype(BF16)
    kbuf[past:s_pad, :] = _pad_rows(kin_ref[0][:, 0:IDX_DIM], s_pad - past)
    t = past + _iota((rows, 1), 0)
    o_ref[0] = _indexer_scores(qi_ref[0], wi_ref[0], kbuf[...], t, _iota((1, s_pad), 1))


def _dsa_decode_scores(qi_bf, wi, ki_new_bf, idx_pool, page_table):
    batch, rows, _ = qi_bf.shape
    _, page, kw = idx_pool.shape
    n_pages = page_table.shape[1]
    past = n_pages * page
    s_pad = past + page
    per_b = lambda *shape: pl.BlockSpec((1,) + shape, lambda b, pt: (b,) + (0,) * len(shape))
    return pl.pallas_call(
        functools.partial(_dsa_decode_score_kernel, n_pages=n_pages, page=page, past=past),
        out_shape=jax.ShapeDtypeStruct((batch, rows, s_pad), F32),
        grid_spec=pltpu.PrefetchScalarGridSpec(
            num_scalar_prefetch=1, grid=(batch,),
            in_specs=[per_b(rows, qi_bf.shape[2]), per_b(rows, LANES), per_b(rows, LANES)]
            + _page_specs(n_pages, page, kw),
            out_specs=per_b(rows, s_pad),
            scratch_shapes=[pltpu.VMEM((s_pad, kw), BF16)]),
        compiler_params=_cparams(1),
        name="dsa_decode_scores",
    )(page_table, qi_bf, wi, ki_new_bf, *([idx_pool] * n_pages))


def _select_kernel(s_ref, o_ref, key_ref, *, top_k):
    o_ref[...] = _topk_bias(s_ref[...], key_ref, top_k)


def _topk_select(scores, top_k, tr=128):
    n, s = scores.shape
    tr = min(tr, n)
    assert n % tr == 0
    return pl.pallas_call(
        functools.partial(_select_kernel, top_k=top_k),
        out_shape=jax.ShapeDtypeStruct((n, s), F32),
        grid=(n // tr,),
        in_specs=[pl.BlockSpec((tr, s), lambda i: (i, 0))],
        out_specs=pl.BlockSpec((tr, s), lambda i: (i, 0)),
        scratch_shapes=[pltpu.VMEM((tr, s), jnp.int32)],
        compiler_params=_cparams(1),
        name="dsa_topk_select",
    )(scores)


def _dsa_decode_kernel(pt_ref, q_ref, bias_ref, kvn_ref, *refs, n_pages, page, past, n_heads):
    page_refs = refs[:n_pages]
    o_ref, kbuf = refs[n_pages:]
    rep = n_heads // N_KV
    s_pad = kbuf.shape[0]
    half = kbuf.shape[1] // 2
    rows = q_ref.shape[1]
    for p in range(n_pages):
        kbuf[p * page:(p + 1) * page, :] = page_refs[p][0].astype(BF16)
    kbuf[past:s_pad, :] = _pad_rows(kvn_ref[0], s_pad - past)
    q = q_ref[0]
    bias = _rep_rows(bias_ref[0], rep)
    outs = []
    for g in range(N_KV):
        qg = _stack_heads(q, g, rep)
        s = _dot_nt(qg, kbuf[:, g * HEAD_DIM:(g + 1) * HEAD_DIM]) + bias
        p, den = _softmax_parts(s)
        o_g = _dot(p.astype(BF16), kbuf[:, half + g * HEAD_DIM:half + (g + 1) * HEAD_DIM]) / _safe_den(den)
        outs += [o_g[r * rows:(r + 1) * rows] for r in range(rep)]
    o_ref[0] = jnp.concatenate(outs, axis=1)


def _dsa_decode_attention(q_bf, bias, kv_new_bf, kv_pool, page_table):
    batch, rows, hd = q_bf.shape
    n_heads = hd // HEAD_DIM
    _, page, kvw = kv_pool.shape
    n_pages = page_table.shape[1]
    past = n_pages * page
    s_pad = past + page
    per_b = lambda *shape: pl.BlockSpec((1,) + shape, lambda b, pt: (b,) + (0,) * len(shape))
    return pl.pallas_call(
        functools.partial(_dsa_decode_kernel, n_pages=n_pages, page=page, past=past, n_heads=n_heads),
        out_shape=jax.ShapeDtypeStruct((batch, rows, hd), F32),
        grid_spec=pltpu.PrefetchScalarGridSpec(
            num_scalar_prefetch=1, grid=(batch,),
            in_specs=[per_b(rows, hd), per_b(rows, s_pad), per_b(rows, kvw)] + _page_specs(n_pages, page, kvw),
            out_specs=per_b(rows, hd),
            scratch_shapes=[pltpu.VMEM((s_pad, kvw), BF16)]),
        compiler_params=_cparams(1),
        name="dsa_decode_attention",
    )(page_table, q_bf, bias, kv_new_bf, *([kv_pool] * n_pages))


def _pad_cols(w, width):
    return jnp.pad(w, ((0, 0), (0, width - w.shape[1])))


def _rope_tables(pos):
    half = HEAD_DIM // 2
    inv_freq = ROPE_THETA ** (-jnp.arange(half, dtype=F32) / half)
    ang = pos.astype(F32)[:, None] * inv_freq[None, :]
    cos, sin = jnp.cos(ang), jnp.sin(ang)
    reps = LANES // HEAD_DIM
    return (jnp.concatenate([cos, cos] * reps, axis=1), jnp.concatenate([-sin, sin] * reps, axis=1))


def _nsa_pack(w_in, hd, kvd, n_heads, decode):
    qd = BF16 if True else F32
    gl0 = hd + 3 * kvd
    gate_cols = np.array([gl0 + h * 3 + c for c in range(3) for h in range(n_heads)])
    w = jnp.concatenate([w_in[:, :gl0], _pad_cols(w_in[:, gate_cols], LANES), w_in[:, gl0 + 3 * n_heads:]], axis=1)
    kv_outs = ((F32, 1.0), (BF16, 1.0))
    segs = [
        _Seg(0, hd, rope=hd, outs=((qd, HEAD_DIM ** -0.5),)),
        _Seg(hd, kvd, rope=kvd // 2, outs=((F32, 1.0),)),
        _Seg(hd + kvd, kvd, rope=kvd // 2, outs=kv_outs),
        _Seg(hd + 2 * kvd, kvd, rope=kvd // 2, outs=kv_outs),
        _Seg(gl0, LANES),
        _Seg(gl0 + LANES, hd),
    ]
    return w.astype(BF16), segs


def _fox_pack(w_in, hd, n_heads):
    fl0 = 3 * hd
    w = jnp.concatenate([w_in[:, :fl0], _pad_cols(w_in[:, fl0:fl0 + n_heads], LANES), w_in[:, fl0 + n_heads:]], axis=1)
    segs = [
        _Seg(0, hd, outs=((BF16, HEAD_DIM ** -0.5),)),
        _Seg(hd, 2 * hd, outs=((F32, 1.0), (BF16, 1.0))),
        _Seg(fl0, LANES, logsig=True),
        _Seg(fl0 + LANES, hd),
    ]
    return w.astype(BF16), segs


def _dsa_pack(w_in, hd, kvd):
    qi0 = hd + kvd
    ki0 = qi0 + IDX_HEADS * IDX_DIM
    wi0 = ki0 + IDX_DIM
    z0 = wi0 + IDX_HEADS
    w = jnp.concatenate([w_in[:, :ki0], _pad_cols(w_in[:, ki0:wi0], LANES), _pad_cols(w_in[:, wi0:z0], LANES),
                         w_in[:, z0:]], axis=1)
    segs = [
        _Seg(0, hd, rope=hd, outs=((BF16, HEAD_DIM ** -0.5),)),
        _Seg(hd, kvd, rope=kvd // 2, outs=((F32, 1.0), (BF16, 1.0))),
        _Seg(qi0, IDX_HEADS * IDX_DIM, rope=IDX_HEADS * IDX_DIM, outs=((BF16, IDX_DIM ** -0.5),)),
        _Seg(ki0, LANES, rope=LANES, outs=((F32, 1.0), (BF16, 1.0))),
        _Seg(ki0 + LANES, LANES),
        _Seg(ki0 + 2 * LANES, hd),
    ]
    return w.astype(BF16), segs


def kernel(x_prompt, x_sample, cache_nsa_cmp_kv, cache_nsa_slc_kv, state_nsa_win_kv, cache_fox_kv, cache_fox_logf,
           cache_dsa_kv, cache_dsa_idx_k, page_table, norm_g, final_norm_g, nsa_w_in, nsa_w_out, fox_w_in, fox_b_f,
           fox_w_out, dsa_w_in, dsa_w_out):
    batch, seq, d_model = x_prompt.shape
    dec_batch, t_new, _ = x_sample.shape
    depth = norm_g.shape[0]
    hd = nsa_w_out.shape[1]
    n_heads = hd // HEAD_DIM
    kvd = 2 * N_KV * HEAD_DIM
    n_pages = page_table.shape[1]
    page = cache_fox_kv.shape[2]
    past = n_pages * page
    n_pool = cache_fox_kv.shape[1]
    assert t_new <= ROWS_PAD and seq % NSA_BLOCK == 0
    tm = 256

    xp = x_prompt.reshape(batch * seq, d_model)
    xs = jnp.pad(x_sample, ((0, 0), (0, ROWS_PAD - t_new), (0, 0))).reshape(dec_batch * ROWS_PAD, d_model)
    cos_p, sin_p = _rope_tables(jnp.arange(seq, dtype=jnp.int32))
    tm_s = min(tm, dec_batch * ROWS_PAD)
    pos_s = past + (jnp.arange(tm_s, dtype=jnp.int32) % ROWS_PAD)
    cos_s, sin_s = _rope_tables(pos_s)
    zero_bias = jnp.zeros((1, LANES), F32)
    per_seq = lambda a: a.reshape(dec_batch, ROWS_PAD, a.shape[-1])
    trim = lambda a: per_seq(a)[:, :t_new]

    outs = {k: [] for k in ("nsa_c_p", "nsa_c_s", "nsa_s_p", "nsa_s_s", "nsa_w_p", "nsa_w_s", "fox_kv_p", "fox_kv_s",
                            "fox_lf_p", "fox_lf_s", "dsa_kv_p", "dsa_kv_s", "dsa_ik_p", "dsa_ik_s")}
    yp_final = ys_final = None
    for i in range(depth):
        kind, j = i % N_MIXERS, i // N_MIXERS
        g = norm_g[i]
        last = i == depth - 1
        if kind == 0:
            w, segs = _nsa_pack(nsa_w_in[j], hd, kvd, n_heads, False)
            w_out = nsa_w_out[j].astype(BF16)
            q, kvc, kvs, kvs_b, kvw, kvw_b, gl, z = _project(xp, g, w, zero_bias, cos_p, sin_p, segs, tm)
            cmp_means = _blockmean_dense(kvc, batch, seq)
            o_p = _nsa_prompt_attention(q, gl, cmp_means, kvs_b, kvw_b, batch, seq)
            z_p = z
            kv6 = lambda a, b: a.reshape(b, -1, 2, N_KV, HEAD_DIM)
            outs["nsa_c_p"].append(kv6(kvc, batch))
            outs["nsa_s_p"].append(kv6(kvs, batch))
            outs["nsa_w_p"].append(kv6(kvw, batch)[:, seq - min(NSA_WINDOW, seq):])

            q, kvc, kvs, kvs_b, kvw, kvw_b, gl, z = _project(xs, g, w, zero_bias, cos_s, sin_s, segs, tm_s)
            cmp_means = _blockmean_paged(cache_nsa_cmp_kv[j].reshape(n_pool, page, kvd), page_table,
                                         (past + t_new) // NSA_BLOCK)
            win_state = state_nsa_win_kv[j].reshape(dec_batch, -1, kvd)
            o_s = _nsa_decode_attention(per_seq(q), per_seq(gl), cmp_means, per_seq(kvs_b), per_seq(kvw_b), win_state,
                                        cache_nsa_slc_kv[j].reshape(n_pool, page, kvd), page_table, t_new)
            o_s = o_s.reshape(dec_batch * ROWS_PAD, hd)
            z_s = z
            outs["nsa_c_s"].append(kv6(trim(kvc), dec_batch))
            outs["nsa_s_s"].append(kv6(trim(kvs), dec_batch))
            keys_w = jnp.concatenate([win_state, trim(kvw)], axis=1)
            outs["nsa_w_s"].append(kv6(keys_w[:, keys_w.shape[1] - min(NSA_WINDOW, keys_w.shape[1]):], dec_batch))
        elif kind == 1:
            w, segs = _fox_pack(fox_w_in[j], hd, n_heads)
            w_out = fox_w_out[j].astype(BF16)
            bias = _pad_cols(fox_b_f[j].reshape(1, n_heads), LANES)
            q, kv, kv_b, lf, z = _project(xp, g, w, bias, cos_p, sin_p, segs, tm)
            lf_p = lf[:, :n_heads].reshape(batch, seq, n_heads)
            o_p = _fox_prompt_attention(q, kv_b, _cumsum_dense(lf_p), batch, seq)
            z_p = z
            outs["fox_kv_p"].append(kv.reshape(batch, seq, 2, n_heads, HEAD_DIM))
            outs["fox_lf_p"].append(lf_p)

            q, kv, kv_b, lf, z = _project(xs, g, w, bias, cos_s, sin_s, segs, tm_s)
            c_all = _cumsum_paged(cache_fox_logf[j], page_table, per_seq(lf))
            o_s = _fox_decode_attention(per_seq(q), per_seq(kv_b), c_all,
                                        cache_fox_kv[j].reshape(n_pool, page, 2 * hd), page_table)
            o_s = o_s.reshape(dec_batch * ROWS_PAD, hd)
            z_s = z
            outs["fox_kv_s"].append(trim(kv).reshape(dec_batch, t_new, 2, n_heads, HEAD_DIM))
            outs["fox_lf_s"].append(trim(lf)[:, :, :n_heads])
        else:
            w, segs = _dsa_pack(dsa_w_in[j], hd, kvd)
            w_out = dsa_w_out[j].astype(BF16)
            q, kv, kv_b, qi, ki, ki_b, wi, z = _project(xp, g, w, zero_bias, cos_p, sin_p, segs, tm)
            o_p = _dsa_prompt_attention(q, qi, wi, ki_b, kv_b, batch, seq)
            z_p = z
            outs["dsa_kv_p"].append(kv.reshape(batch, seq, 2, N_KV, HEAD_DIM))
            outs["dsa_ik_p"].append(ki[:, :IDX_DIM].reshape(batch, seq, IDX_DIM))

            q, kv, kv_b, qi, ki, ki_b, wi, z = _project(xs, g, w, zero_bias, cos_s, sin_s, segs, tm_s)
            scores = _dsa_decode_scores(per_seq(qi), per_seq(wi), per_seq(ki_b), cache_dsa_idx_k[j], page_table)
            s_pad = scores.shape[2]
            sel_bias = _topk_select(scores.reshape(dec_batch * ROWS_PAD, s_pad), min(DSA_TOP_K, (past + t_new) // 4))
            o_s = _dsa_decode_attention(per_seq(q), sel_bias.reshape(dec_batch, ROWS_PAD, s_pad), per_seq(kv_b),
                                        cache_dsa_kv[j].reshape(n_pool, page, kvd), page_table)
            o_s = o_s.reshape(dec_batch * ROWS_PAD, hd)
            z_s = z
            outs["dsa_kv_s"].append(trim(kv).reshape(dec_batch, t_new, 2, N_KV, HEAD_DIM))
            outs["dsa_ik_s"].append(trim(ki)[:, :, :IDX_DIM])
        fg = final_norm_g if last else None
        res_p = _out_project(o_p, z_p, xp, w_out, fg, tm)
        res_s = _out_project(o_s, z_s, xs, w_out, fg, tm_s)
        xp, xs = res_p[0], res_s[0]
        if last:
            yp_final, ys_final = res_p[1], res_s[1]

    y_prompt = yp_final.reshape(batch, seq, d_model)
    y_sample = ys_final.reshape(dec_batch, ROWS_PAD, d_model)[:, :t_new]
    st = jnp.stack
    return (y_prompt, y_sample,
            st(outs["nsa_c_p"]), st(outs["nsa_c_s"]), st(outs["nsa_s_p"]), st(outs["nsa_s_s"]),
            st(outs["nsa_w_p"]), st(outs["nsa_w_s"]),
            st(outs["fox_kv_p"]), st(outs["fox_kv_s"]), st(outs["fox_lf_p"]), st(outs["fox_lf_s"]),
            st(outs["dsa_kv_p"]), st(outs["dsa_kv_s"]), st(outs["dsa_ik_p"]), st(outs["dsa_ik_s"]))
```

```python
import functools

import numpy as np
import jax
import jax.numpy as jnp
from jax import lax
from jax.experimental import pallas as pl
from jax.experimental.pallas import tpu as pltpu

F32 = jnp.float32
BF16 = jnp.bfloat16
NEG_INF = float("-inf")
POS_INF = float("inf")

HEAD_DIM = 64
N_KV = 4
ROPE_THETA = 10000.0
RMS_EPS = 1e-6
NSA_BLOCK = 64
NSA_TOP_N = 16
NSA_WINDOW = 512
IDX_HEADS = 8
IDX_DIM = 64
DSA_TOP_K = 256
N_MIXERS = 3

LANES = 128
SUBLANES = 8
SUBLANES_BF16 = 16
VMEM_LIMIT = 56 * 1024 * 1024

ROWS_PAD = SUBLANES_BF16
M_INIT = -1e30


def _cparams(n_axes):
    return pltpu.CompilerParams(dimension_semantics=("arbitrary",) * n_axes,
                                vmem_limit_bytes=VMEM_LIMIT)


def _dot(a, b):
    return jnp.dot(a, b, preferred_element_type=F32)


def _dot_nt(a, b):
    return lax.dot_general(a, b, (((1,), (1,)), ((), ())), preferred_element_type=F32)


def _iota(shape, dim):
    return lax.broadcasted_iota(jnp.int32, shape, dim)


def _log2(n):
    assert n & (n - 1) == 0
    return n.bit_length() - 1


def _ones_where(mask):
    return jnp.where(mask, 1.0, 0.0)


def _bias_where(*masks):
    bias = 0.0
    for mask in reversed(masks):
        bias = jnp.where(mask, bias, NEG_INF)
    return bias


def _softmax_parts(s):
    m = jnp.max(s, axis=-1, keepdims=True)
    m = jnp.where(m > NEG_INF, m, 0.0)
    p = jnp.exp(s - m)
    return p, jnp.sum(p, axis=-1, keepdims=True)


def _softmax_parts2(s_a, s_b):
    m = jnp.maximum(jnp.max(s_a, axis=-1, keepdims=True), jnp.max(s_b, axis=-1, keepdims=True))
    m = jnp.where(m > NEG_INF, m, 0.0)
    p_a = jnp.exp(s_a - m)
    p_b = jnp.exp(s_b - m)
    return p_a, p_b, jnp.sum(p_a, axis=-1, keepdims=True) + jnp.sum(p_b, axis=-1, keepdims=True)


def _safe_den(den):
    return jnp.where(den > 0, den, 1.0)


def _online_update(s, vv, m_i, l_i, acc):
    m_new = jnp.maximum(m_i, jnp.max(s, axis=-1, keepdims=True))
    alpha = jnp.exp(m_i - m_new)
    p = jnp.exp(s - m_new)
    l_new = alpha * l_i + jnp.sum(p, axis=-1, keepdims=True)
    acc_new = alpha * acc + _dot(p.astype(BF16), vv)
    return m_new, l_new, acc_new


def _sigmoid(x):
    return 1.0 / (1.0 + jnp.exp(-x))


def _split3(x):
    hi = x.astype(BF16)
    r1 = x - hi.astype(F32)
    mid = r1.astype(BF16)
    lo = (r1 - mid.astype(F32)).astype(BF16)
    return hi, mid, lo


def _pad_rows(rows, total):
    return jnp.concatenate([rows, jnp.zeros((total - rows.shape[0], rows.shape[1]), rows.dtype)], axis=0)


class _Seg:
    def __init__(self, col, width, rope=0, logsig=False, outs=((F32, 1.0),)):
        self.col, self.width, self.rope, self.logsig, self.outs = col, width, rope, logsig, outs


def _rope(y, cos, sin):
    w = y.shape[1]
    reps = w // LANES
    cw = jnp.concatenate([cos] * reps, axis=1) if reps > 1 else cos
    sw = jnp.concatenate([sin] * reps, axis=1) if reps > 1 else sin
    half = HEAD_DIM // 2
    first = (_iota(y.shape, 1) & (HEAD_DIM - 1)) < half
    rot = jnp.where(first, pltpu.roll(y, w - half, 1), pltpu.roll(y, half, 1))
    return y * cw + rot * sw


def _log_sigmoid(x):
    return -(jnp.maximum(-x, 0.0) + jnp.log1p(jnp.exp(-jnp.abs(x))))


def _proj_kernel(x_ref, g_ref, w_ref, cos_ref, sin_ref, b_ref, *out_refs, segs):
    x = x_ref[...]
    h = x * lax.rsqrt(jnp.mean(x * x, axis=-1, keepdims=True) + RMS_EPS)
    hb = (h * g_ref[...]).astype(BF16)
    oi = 0
    for seg in segs:
        y = _dot(hb, w_ref[:, seg.col:seg.col + seg.width])
        if seg.rope == seg.width:
            y = _rope(y, cos_ref[...], sin_ref[...])
        elif seg.rope:
            y = jnp.concatenate([_rope(y[:, :seg.rope], cos_ref[...], sin_ref[...]), y[:, seg.rope:]], axis=1)
        if seg.logsig:
            y = _log_sigmoid(y + b_ref[...])
        for dtype, scale in seg.outs:
            out_refs[oi][...] = (y if scale == 1.0 else y * scale).astype(dtype)
            oi += 1


def _project(x, g, w_bf16, bias, cos, sin, segs, tm):
    n, d = x.shape
    assert n % tm == 0 and cos.shape[0] % tm == 0
    n_pos_tiles = cos.shape[0] // tm
    out_shape, out_specs = [], []
    for seg in segs:
        for dtype, _ in seg.outs:
            out_shape.append(jax.ShapeDtypeStruct((n, seg.width), dtype))
            out_specs.append(pl.BlockSpec((tm, seg.width), lambda i: (i, 0)))
    return pl.pallas_call(
        functools.partial(_proj_kernel, segs=segs),
        out_shape=out_shape,
        grid=(n // tm,),
        in_specs=[
            pl.BlockSpec((tm, d), lambda i: (i, 0)),
            pl.BlockSpec((1, d), lambda i: (0, 0)),
            pl.BlockSpec(w_bf16.shape, lambda i: (0, 0)),
            pl.BlockSpec((tm, LANES), lambda i: (i % n_pos_tiles, 0)),
            pl.BlockSpec((tm, LANES), lambda i: (i % n_pos_tiles, 0)),
            pl.BlockSpec((1, LANES), lambda i: (0, 0)),
        ],
        out_specs=out_specs,
        compiler_params=_cparams(1),
        name="rmsnorm_proj",
    )(x, g.reshape(1, d), w_bf16, cos, sin, bias)


def _out_kernel(o_ref, z_ref, x_ref, w_ref, g_ref, y_ref, *yn_ref):
    z = z_ref[...]
    a = o_ref[...] * (z * _sigmoid(z))
    y = x_ref[...] + _dot(a.astype(BF16), w_ref[...])
    y_ref[...] = y
    if yn_ref:
        yn_ref[0][...] = y * lax.rsqrt(jnp.mean(y * y, axis=-1, keepdims=True) + RMS_EPS) * g_ref[...]


def _out_project(o, z, x, w_bf16, final_g, tm):
    n, d = x.shape
    hd = o.shape[1]
    row = lambda w: pl.BlockSpec((tm, w), lambda i: (i, 0))
    final = final_g is not None
    g = final_g.reshape(1, d) if final else jnp.zeros((1, d), F32)
    out_shape = [jax.ShapeDtypeStruct((n, d), F32)] * (2 if final else 1)
    return pl.pallas_call(
        _out_kernel,
        out_shape=out_shape,
        grid=(n // tm,),
        in_specs=[row(hd), row(hd), row(d), pl.BlockSpec((hd, d), lambda i: (0, 0)),
                  pl.BlockSpec((1, d), lambda i: (0, 0))],
        out_specs=[row(d)] * (2 if final else 1),
        compiler_params=_cparams(1),
        name="gated_out_proj",
    )(o, z, x, w_bf16, g)


def _paged_view(cache):
    perm = (0, 1) + tuple(range(3, cache.ndim)) + (2,)
    return jnp.transpose(cache, perm).reshape(cache.shape[0], cache.shape[1], -1, cache.shape[2])


def _page_specs(n_pages, feat, page, layer):
    return [pl.BlockSpec((1, 1, feat, page), lambda b, pt, p=p: (layer, pt[b, p], 0, 0)) for p in range(n_pages)]


def _per_seq(*shape):
    return pl.BlockSpec((1,) + shape, lambda b, pt: (b,) + (0,) * len(shape))


def _blockmean_kernel(x_ref, o_ref, *, n_blocks):
    w = x_ref.shape[1]
    means = jnp.sum(x_ref[...].reshape(n_blocks, NSA_BLOCK, w), axis=1) * (1.0 / NSA_BLOCK)
    o_ref[0] = _pad_rows(means, LANES)


def _blockmean_dense(kvc, batch, seq):
    w = kvc.shape[1]
    n_blocks = seq // NSA_BLOCK
    assert n_blocks % SUBLANES == 0 and n_blocks <= LANES
    return pl.pallas_call(
        functools.partial(_blockmean_kernel, n_blocks=n_blocks),
        out_shape=jax.ShapeDtypeStruct((batch, LANES, w), F32),
        grid=(batch,),
        in_specs=[pl.BlockSpec((seq, w), lambda b: (b, 0))],
        out_specs=pl.BlockSpec((1, LANES, w), lambda b: (b, 0, 0)),
        compiler_params=_cparams(1),
        name="nsa_blockmean",
    )(kvc)


def _blockmean_paged_kernel(pt_ref, *refs, n_pages, n_blocks):
    page_refs, o_ref = refs[:n_pages], refs[n_pages]
    rows_t = jnp.concatenate([r[0, 0] for r in page_refs], axis=1)
    past = rows_t.shape[1]
    blk = _iota((LANES, past), 0)
    member = jnp.where((_iota((LANES, past), 1) >> _log2(NSA_BLOCK)) == blk, _ones_where(blk < n_blocks), 0.0)
    member = member.astype(BF16)
    hi, mid, lo = _split3(rows_t)
    o_ref[0] = (_dot_nt(member, hi) + _dot_nt(member, mid) + _dot_nt(member, lo)) * (1.0 / NSA_BLOCK)


def _blockmean_paged(pool_t, layer, page_table, n_blocks):
    _, _, w, page = pool_t.shape
    batch, n_pages = page_table.shape
    assert n_blocks * NSA_BLOCK <= n_pages * page and n_blocks <= LANES
    return pl.pallas_call(
        functools.partial(_blockmean_paged_kernel, n_pages=n_pages, n_blocks=n_blocks),
        out_shape=jax.ShapeDtypeStruct((batch, LANES, w), F32),
        grid_spec=pltpu.PrefetchScalarGridSpec(
            num_scalar_prefetch=1, grid=(batch,),
            in_specs=_page_specs(n_pages, w, page, layer),
            out_specs=_per_seq(LANES, w)),
        compiler_params=_cparams(1),
        name="nsa_blockmean_paged",
    )(page_table, *([pool_t] * n_pages))


def _stack_heads(q, g, rep):
    return jnp.concatenate(
        [q[:, (g * rep + r) * HEAD_DIM:(g * rep + r + 1) * HEAD_DIM] for r in range(rep)], axis=0)


def _rep_rows(a, rep):
    return jnp.concatenate([a] * rep, axis=0)


def _nsa_compressed(qg, cmp_bf, g, t, n_cmp, rows, rep, kv_width):
    half = kv_width // 2
    ck = cmp_bf[:, g * HEAD_DIM:(g + 1) * HEAD_DIM]
    cv = cmp_bf[:, half + g * HEAD_DIM:half + (g + 1) * HEAD_DIM]
    blk = _iota((1, LANES), 1)
    visible = _bias_where(blk < n_cmp, (blk + 1) * NSA_BLOCK - 1 <= t)
    s = _dot_nt(qg, ck) + _rep_rows(visible, rep)
    p, den = _softmax_parts(s)
    p = p / _safe_den(den)
    o_cmp = _dot(p.astype(BF16), cv)
    imp = p[0:rows]
    for r in range(1, rep):
        imp = imp + p[r * rows:(r + 1) * rows]
    return o_cmp, imp


def _nsa_select(imps, t0, rows, n_sel):
    n_grp = len(imps)
    n_pad = -(-n_sel // SUBLANES) * SUBLANES
    width = n_grp * LANES
    imp_t = jnp.concatenate(
        [(imp if rows == LANES else _pad_rows(imp, LANES)).T[0:n_pad, :] for imp in imps], axis=1)
    cur = (t0 + (_iota((1, width), 1) & (LANES - 1))) >> _log2(NSA_BLOCK)
    blk = _iota((n_pad, width), 0)
    score = jnp.where(blk == cur, POS_INF, jnp.where(blk == 0, POS_INF, jnp.where(blk < cur, imp_t, NEG_INF)))
    score = jnp.where(blk < n_sel, score, NEG_INF)
    tiles = [score[SUBLANES * r:SUBLANES * (r + 1)] for r in range(n_pad // SUBLANES)]
    ranks = [jnp.zeros((SUBLANES, width), F32) for _ in tiles]
    sub = _iota((SUBLANES, width), 0)
    for m in range(n_sel):
        sm = tiles[m // SUBLANES][m % SUBLANES:m % SUBLANES + 1, :]
        for r, tile in enumerate(tiles):
            if SUBLANES * r > m:
                beats = _ones_where(sm >= tile)
            elif SUBLANES * (r + 1) <= m:
                beats = _ones_where(sm > tile)
            else:
                loses_tie = _ones_where(sub > m - SUBLANES * r)
                beats = jnp.where(sm > tile, 1.0, jnp.where(sm == tile, loses_tie, 0.0))
            ranks[r] = ranks[r] + beats
    top_n = min(NSA_TOP_N, n_sel)
    sel_t = jnp.concatenate(
        [jnp.where(rank < top_n, _ones_where(tile > NEG_INF), 0.0) for rank, tile in zip(ranks, tiles)]
        + [jnp.zeros((LANES - n_pad, width), F32)], axis=0)
    return [sel_t[:, g * LANES:(g + 1) * LANES].T[0:rows, :] for g in range(n_grp)]


def _expand_blocks(sel, kpos):
    onehot = (kpos >> _log2(NSA_BLOCK)) == _iota((LANES, kpos.shape[1]), 0)
    return _dot(sel.astype(BF16), _ones_where(onehot).astype(BF16))


def _nsa_combine(gates, g, rep, rows, n_heads, o_cmp, o_sel, o_win):
    outs = []
    for r in range(rep):
        h = g * rep + r
        sl = slice(r * rows, (r + 1) * rows)
        outs.append(gates[:, h:h + 1] * o_cmp[sl]
                    + gates[:, n_heads + h:n_heads + h + 1] * o_sel[sl]
                    + gates[:, 2 * n_heads + h:2 * n_heads + h + 1] * o_win[sl])
    return outs


def _nsa_prompt_kernel(q_ref, gl_ref, cmp_ref, ks_ref, kw_ref, o_ref, *, tq, kc, seq, n_heads):
    rep = n_heads // N_KV
    kv_width = ks_ref.shape[1]
    half = kv_width // 2
    q0 = pl.program_id(1) * tq
    t = q0 + _iota((tq, 1), 0)
    n_blk = seq // NSA_BLOCK
    q = q_ref[...]
    gates = _sigmoid(gl_ref[...])
    cmp_bf = cmp_ref[0].astype(BF16)
    n_chunks = (q0 + tq + kc - 1) // kc
    win_lo = pl.multiple_of(jnp.maximum(q0 - NSA_WINDOW, 0), tq)
    win_len = NSA_WINDOW + tq
    wpos = win_lo + _iota((1, win_len), 1)
    in_window = _rep_rows(_bias_where(wpos <= t, wpos > t - NSA_WINDOW), rep)
    qgs = [_stack_heads(q, g, rep) for g in range(N_KV)]
    cmp_out = [_nsa_compressed(qgs[g], cmp_bf, g, t, n_blk, tq, rep, kv_width) for g in range(N_KV)]
    sels = _nsa_select([imp for _, imp in cmp_out], q0, tq, n_blk)
    outs = []
    for g in range(N_KV):
        qg, sel, o_cmp = qgs[g], sels[g], cmp_out[g][0]

        def sel_body(c, carry, qg=qg, sel=sel, g=g):
            k0 = pl.multiple_of(c * kc, kc)
            kk = ks_ref[pl.ds(k0, kc), g * HEAD_DIM:(g + 1) * HEAD_DIM]
            vv = ks_ref[pl.ds(k0, kc), half + g * HEAD_DIM:half + (g + 1) * HEAD_DIM]
            kpos = k0 + _iota((1, kc), 1)
            allowed = _bias_where(_expand_blocks(sel, kpos) > 0.5, kpos <= t)
            s = _dot_nt(qg, kk) + _rep_rows(allowed, rep)
            return _online_update(s, vv, *carry)

        init = (jnp.full((rep * tq, 1), M_INIT, F32), jnp.zeros((rep * tq, 1), F32),
                jnp.zeros((rep * tq, HEAD_DIM), F32))
        _, l_i, acc = lax.fori_loop(0, n_chunks, sel_body, init)
        o_sel = acc / _safe_den(l_i)

        kk = kw_ref[pl.ds(win_lo, win_len), g * HEAD_DIM:(g + 1) * HEAD_DIM]
        vv = kw_ref[pl.ds(win_lo, win_len), half + g * HEAD_DIM:half + (g + 1) * HEAD_DIM]
        p, den = _softmax_parts(_dot_nt(qg, kk) + in_window)
        o_win = _dot(p.astype(BF16), vv) / _safe_den(den)
        outs += _nsa_combine(gates, g, rep, tq, n_heads, o_cmp, o_sel, o_win)
    o_ref[...] = jnp.concatenate(outs, axis=1)


def _nsa_prompt_attention(q_bf, gl, cmp_means, ks_bf, kw_bf, batch, seq, kc=512):
    n, hd = q_bf.shape
    n_heads = hd // HEAD_DIM
    kvw = ks_bf.shape[1]
    tq = LANES
    assert seq % kc == 0 and seq % tq == 0 and NSA_WINDOW % tq == 0 and NSA_WINDOW + tq <= seq
    assert seq // NSA_BLOCK <= LANES and seq % NSA_BLOCK == 0
    nq = seq // tq
    return pl.pallas_call(
        functools.partial(_nsa_prompt_kernel, tq=tq, kc=kc, seq=seq, n_heads=n_heads),
        out_shape=jax.ShapeDtypeStruct((n, hd), F32),
        grid=(batch, nq),
        in_specs=[
            pl.BlockSpec((tq, hd), lambda b, i: (b * nq + i, 0)),
            pl.BlockSpec((tq, LANES), lambda b, i: (b * nq + i, 0)),
            pl.BlockSpec((1, LANES, kvw), lambda b, i: (b, 0, 0)),
            pl.BlockSpec((seq, kvw), lambda b, i: (b, 0)),
            pl.BlockSpec((seq, kvw), lambda b, i: (b, 0)),
        ],
        out_specs=pl.BlockSpec((tq, hd), lambda b, i: (b * nq + i, 0)),
        compiler_params=_cparams(2),
        name="nsa_prompt_attention",
    )(q_bf, gl, cmp_means, ks_bf, kw_bf)


def _nsa_decode_kernel(pt_ref, q_ref, gl_ref, cmp_ref, ksn_ref, kwn_ref, win_ref, *refs,
                       n_pages, page, t_new, n_heads):
    page_refs = refs[:n_pages]
    o_ref, kbuf, wbuf = refs[n_pages:]
    rep = n_heads // N_KV
    kv_width, past = kbuf.shape
    half = kv_width // 2
    rows = q_ref.shape[1]
    n_cmp = (past + t_new) // NSA_BLOCK
    n_sel = -(-(past + t_new) // NSA_BLOCK)
    w_state = wbuf.shape[1]
    for p in range(n_pages):
        kbuf[:, p * page:(p + 1) * page] = page_refs[p][0, 0].astype(BF16)
    wbuf[...] = win_ref[0, 0].astype(BF16)
    ks_new = _pad_rows(ksn_ref[0], LANES)
    kw_new = _pad_rows(kwn_ref[0], LANES)

    t = past + _iota((rows, 1), 0)
    q = q_ref[0]
    gates = _sigmoid(gl_ref[0])
    cmp_bf = cmp_ref[0].astype(BF16)
    kpos = _iota((1, past), 1)
    npos = past + _iota((1, LANES), 1)
    kpos_w = (past - w_state) + _iota((1, w_state), 1)
    okw = _rep_rows(_bias_where(kpos_w <= t, kpos_w > t - NSA_WINDOW, kpos_w >= 0), rep)
    okw_new = _rep_rows(_bias_where(npos <= t, npos > t - NSA_WINDOW), rep)
    qgs = [_stack_heads(q, g, rep) for g in range(N_KV)]
    cmp_out = [_nsa_compressed(qgs[g], cmp_bf, g, t, n_cmp, rows, rep, kv_width) for g in range(N_KV)]
    sels = _nsa_select([imp for _, imp in cmp_out], past, rows, n_sel)
    outs = []
    for g in range(N_KV):
        qg, sel, o_cmp = qgs[g], sels[g], cmp_out[g][0]
        k_rows = slice(g * HEAD_DIM, (g + 1) * HEAD_DIM)
        v_rows = slice(half + g * HEAD_DIM, half + (g + 1) * HEAD_DIM)
        ok = _rep_rows(_bias_where(_expand_blocks(sel, kpos) > 0.5, kpos <= t), rep)
        ok_new = _rep_rows(_bias_where(_expand_blocks(sel, npos) > 0.5, npos <= t), rep)
        p_a, p_b, den = _softmax_parts2(_dot(qg, kbuf[k_rows, :]) + ok, _dot_nt(qg, ks_new[:, k_rows]) + ok_new)
        o_sel = (_dot_nt(p_a.astype(BF16), kbuf[v_rows, :]) + _dot(p_b.astype(BF16), ks_new[:, v_rows])) / _safe_den(den)
        p_a, p_b, den = _softmax_parts2(_dot(qg, wbuf[k_rows, :]) + okw, _dot_nt(qg, kw_new[:, k_rows]) + okw_new)
        o_win = (_dot_nt(p_a.astype(BF16), wbuf[v_rows, :]) + _dot(p_b.astype(BF16), kw_new[:, v_rows])) / _safe_den(den)
        outs += _nsa_combine(gates, g, rep, rows, n_heads, o_cmp, o_sel, o_win)
    o_ref[0] = jnp.concatenate(outs, axis=1)


def _nsa_decode_attention(q_bf, gl, cmp_means, ks_new, kw_new, win_t, slc_t, layer, page_table, t_new):
    batch, rows, hd = q_bf.shape
    n_heads = hd // HEAD_DIM
    _, _, kvw, page = slc_t.shape
    n_pages = page_table.shape[1]
    past = n_pages * page
    w_state = win_t.shape[3]
    assert past % NSA_BLOCK == 0 and -(-(past + t_new) // NSA_BLOCK) <= LANES and rows <= LANES
    return pl.pallas_call(
        functools.partial(_nsa_decode_kernel, n_pages=n_pages, page=page, t_new=t_new, n_heads=n_heads),
        out_shape=jax.ShapeDtypeStruct((batch, rows, hd), F32),
        grid_spec=pltpu.PrefetchScalarGridSpec(
            num_scalar_prefetch=1, grid=(batch,),
            in_specs=[_per_seq(rows, hd), _per_seq(rows, LANES), _per_seq(LANES, kvw), _per_seq(rows, kvw),
                      _per_seq(rows, kvw),
                      pl.BlockSpec((1, 1, kvw, w_state), lambda b, pt: (layer, b, 0, 0))]
            + _page_specs(n_pages, kvw, page, layer),
            out_specs=_per_seq(rows, hd),
            scratch_shapes=[pltpu.VMEM((kvw, past), BF16), pltpu.VMEM((kvw, w_state), BF16)]),
        compiler_params=_cparams(1),
        name="nsa_decode_attention",
    )(page_table, q_bf, gl, cmp_means, ks_new, kw_new, win_t, *([slc_t] * n_pages))


def _cumsum_rows(x, tri_bf):
    hi, mid, lo = _split3(x)
    return _dot(tri_bf, hi) + _dot(tri_bf, mid) + _dot(tri_bf, lo)


def _cumsum_lanes(x, tri_bf):
    hi, mid, lo = _split3(x)
    return _dot(hi, tri_bf) + _dot(mid, tri_bf) + _dot(lo, tri_bf)


def _lower_tri(n):
    return _ones_where(_iota((n, n), 0) >= _iota((n, n), 1)).astype(BF16)


def _upper_tri(n):
    return _ones_where(_iota((n, n), 0) <= _iota((n, n), 1)).astype(BF16)


def _cumsum_kernel(x_ref, o_ref, *, chunk):
    n = x_ref.shape[1]
    tri = _lower_tri(chunk)
    carry = jnp.zeros((1, x_ref.shape[2]), F32)
    for c in range(n // chunk):
        cs = _cumsum_rows(x_ref[0, c * chunk:(c + 1) * chunk, :], tri) + carry
        o_ref[0, c * chunk:(c + 1) * chunk, :] = cs
        carry = cs[chunk - 1:chunk, :]


def _cumsum_dense(lf, chunk=256):
    batch, seq, h = lf.shape
    assert seq % chunk == 0
    return pl.pallas_call(
        functools.partial(_cumsum_kernel, chunk=chunk),
        out_shape=jax.ShapeDtypeStruct(lf.shape, F32),
        grid=(batch,),
        in_specs=[pl.BlockSpec((1, seq, h), lambda b: (b, 0, 0))],
        out_specs=pl.BlockSpec((1, seq, h), lambda b: (b, 0, 0)),
        compiler_params=_cparams(1),
        name="fox_cumsum",
    )(lf)


def _cumsum_paged_kernel(pt_ref, new_ref, *refs, n_pages, page):
    page_refs = refs[:n_pages]
    ck_ref, cn_ref, cnt_ref = refs[n_pages:]
    n_heads = ck_ref.shape[1]
    tri_u = _upper_tri(page)
    carry = jnp.zeros((n_heads, 1), F32)
    for p in range(n_pages):
        cs = _cumsum_lanes(page_refs[p][0, 0], tri_u) + carry
        ck_ref[0, :, p * page:(p + 1) * page] = cs
        carry = cs[:, page - 1:page]
    diag = _iota((n_heads, LANES), 0) == _iota((n_heads, LANES), 1)
    total = jnp.sum(jnp.where(diag, carry, 0.0), axis=0, keepdims=True)
    c_new = _cumsum_rows(_pad_rows(new_ref[0], LANES), _lower_tri(LANES)) + total
    cn_ref[0] = c_new[0:cn_ref.shape[1], :]
    cnt_ref[0] = c_new.T[0:n_heads, :]


def _cumsum_paged(logf_t, layer, page_table, lf_new):
    _, _, h, page = logf_t.shape
    batch, n_pages = page_table.shape
    rows = lf_new.shape[1]
    assert h <= LANES and rows <= LANES
    return pl.pallas_call(
        functools.partial(_cumsum_paged_kernel, n_pages=n_pages, page=page),
        out_shape=[jax.ShapeDtypeStruct((batch, h, n_pages * page), F32),
                   jax.ShapeDtypeStruct((batch, rows, LANES), F32),
                   jax.ShapeDtypeStruct((batch, h, LANES), F32)],
        grid_spec=pltpu.PrefetchScalarGridSpec(
            num_scalar_prefetch=1, grid=(batch,),
            in_specs=[_per_seq(rows, LANES)] + _page_specs(n_pages, h, page, layer),
            out_specs=[_per_seq(h, n_pages * page), _per_seq(rows, LANES), _per_seq(h, LANES)]),
        compiler_params=_cparams(1),
        name="fox_cumsum_paged",
    )(page_table, lf_new, *([logf_t] * n_pages))


HEADS_PER_BLOCK = LANES // HEAD_DIM


def _fox_prompt_kernel(q_ref, k_ref, v_ref, cq_ref, ck_ref, o_ref, *, tq, kc):
    q0 = pl.program_id(2) * tq
    t = q0 + _iota((tq, 1), 0)
    n_chunks = (q0 + tq + kc - 1) // kc
    outs = []
    for j in range(HEADS_PER_BLOCK):
        cols = slice(j * HEAD_DIM, (j + 1) * HEAD_DIM)
        qh = q_ref[:, cols]
        cq = cq_ref[0, 0][:, j:j + 1]

        def body(c, carry, qh=qh, cq=cq, cols=cols, j=j):
            k0 = pl.multiple_of(c * kc, kc)
            kk = k_ref[pl.ds(k0, kc), cols]
            vv = v_ref[pl.ds(k0, kc), cols]
            ck = ck_ref[0, 0, j:j + 1, pl.ds(k0, kc)]
            kpos = k0 + _iota((1, kc), 1)
            s = jnp.where(kpos <= t, _dot_nt(qh, kk) + (cq - ck), NEG_INF)
            return _online_update(s, vv, *carry)

        init = (jnp.full((tq, 1), M_INIT, F32), jnp.zeros((tq, 1), F32), jnp.zeros((tq, HEAD_DIM), F32))
        _, l_i, acc = lax.fori_loop(0, n_chunks, body, init)
        outs.append(acc / _safe_den(l_i))
    o_ref[...] = jnp.concatenate(outs, axis=1)


def _fox_prompt_attention(q_bf, kv_bf, csum, batch, seq, tq=256, kc=512):
    n, hd = q_bf.shape
    n_hb = hd // LANES
    nq = seq // tq
    assert seq % tq == 0 and seq % kc == 0
    c_pairs = csum.reshape(batch, seq, n_hb, HEADS_PER_BLOCK)
    cq = jnp.transpose(c_pairs, (0, 2, 1, 3))
    ck = jnp.transpose(c_pairs, (0, 2, 3, 1))
    return pl.pallas_call(
        functools.partial(_fox_prompt_kernel, tq=tq, kc=kc),
        out_shape=jax.ShapeDtypeStruct((n, hd), F32),
        grid=(batch, n_hb, nq),
        in_specs=[
            pl.BlockSpec((tq, LANES), lambda b, h, i: (b * nq + i, h)),
            pl.BlockSpec((seq, LANES), lambda b, h, i: (b, h)),
            pl.BlockSpec((seq, LANES), lambda b, h, i: (b, n_hb + h)),
            pl.BlockSpec((1, 1, tq, HEADS_PER_BLOCK), lambda b, h, i: (b, h, i, 0)),
            pl.BlockSpec((1, 1, HEADS_PER_BLOCK, seq), lambda b, h, i: (b, h, 0, 0)),
        ],
        out_specs=pl.BlockSpec((tq, LANES), lambda b, h, i: (b * nq + i, h)),
        compiler_params=_cparams(3),
        name="fox_prompt_attention",
    )(q_bf, kv_bf, kv_bf, cq, ck)


def _fox_decode_kernel(pt_ref, q_ref, kvn_ref, cq_ref, cnt_ref, ck_ref, *refs, n_pages, page, n_heads):
    page_refs = refs[:n_pages]
    o_ref, kbuf = refs[n_pages:]
    past = kbuf.shape[1]
    hd = n_heads * HEAD_DIM
    rows = q_ref.shape[1]
    for p in range(n_pages):
        kbuf[:, p * page:(p + 1) * page] = page_refs[p][0, 0].astype(BF16)
    kv_new = _pad_rows(kvn_ref[0], LANES)
    t = past + _iota((rows, 1), 0)
    visible = _iota((1, past), 1) <= t
    visible_new = past + _iota((1, LANES), 1) <= t
    q = q_ref[0]
    cq = cq_ref[0]
    outs = []
    for h in range(n_heads):
        k_rows = slice(h * HEAD_DIM, (h + 1) * HEAD_DIM)
        v_rows = slice(hd + h * HEAD_DIM, hd + (h + 1) * HEAD_DIM)
        qh = q[:, k_rows]
        s_a = _dot(qh, kbuf[k_rows, :]) + (cq[:, h:h + 1] - ck_ref[0, h:h + 1, :])
        s_b = _dot_nt(qh, kv_new[:, k_rows]) + (cq[:, h:h + 1] - cnt_ref[0, h:h + 1, :])
        p_a, p_b, den = _softmax_parts2(jnp.where(visible, s_a, NEG_INF), jnp.where(visible_new, s_b, NEG_INF))
        outs.append((_dot_nt(p_a.astype(BF16), kbuf[v_rows, :]) + _dot(p_b.astype(BF16), kv_new[:, v_rows]))
                    / _safe_den(den))
    o_ref[0] = jnp.concatenate(outs, axis=1)


def _fox_decode_attention(q_bf, kv_new_bf, c_keys, c_new, c_new_t, kv_t, layer, page_table):
    batch, rows, hd = q_bf.shape
    n_heads = hd // HEAD_DIM
    _, _, kvw, page = kv_t.shape
    n_pages = page_table.shape[1]
    past = n_pages * page
    return pl.pallas_call(
        functools.partial(_fox_decode_kernel, n_pages=n_pages, page=page, n_heads=n_heads),
        out_shape=jax.ShapeDtypeStruct((batch, rows, hd), F32),
        grid_spec=pltpu.PrefetchScalarGridSpec(
            num_scalar_prefetch=1, grid=(batch,),
            in_specs=[_per_seq(rows, hd), _per_seq(rows, kvw), _per_seq(rows, LANES), _per_seq(n_heads, LANES),
                      _per_seq(n_heads, past)] + _page_specs(n_pages, kvw, page, layer),
            out_specs=_per_seq(rows, hd),
            scratch_shapes=[pltpu.VMEM((kvw, past), BF16)]),
        compiler_params=_cparams(1),
        name="fox_decode_attention",
    )(page_table, q_bf, kv_new_bf, c_new, c_new_t, c_keys, *([kv_t] * n_pages))


def _order_key(bits):
    return bits ^ ((bits >> 31) & 0x7FFFFFFF)


_NEG_INF_KEY = int(_order_key(np.array(-np.inf, np.float32).view(np.int32)))


def _lane_fold(x):
    acc = x[:, 0:LANES]
    for b in range(1, x.shape[1] // LANES):
        acc = acc + x[:, b * LANES:(b + 1) * LANES]
    return acc


def _topk_bias(score_ref, key_ref, top_k, n_chunks, kc):
    rows = score_ref.shape[0]
    assert kc >= top_k and kc % LANES == 0

    def chunk_sum(fn):
        def body(c, acc):
            k0 = pl.multiple_of(c * kc, kc)
            return acc + _lane_fold(fn(k0))
        acc = lax.fori_loop(0, n_chunks, body, jnp.zeros((rows, LANES), F32))
        return jnp.sum(acc, axis=-1, keepdims=True)

    def make_keys(c, carry):
        k0 = pl.multiple_of(c * kc, kc)
        key_ref[:, pl.ds(k0, kc)] = _order_key(lax.bitcast_convert_type(score_ref[:, pl.ds(k0, kc)] + 0.0, jnp.int32))
        return carry

    lax.fori_loop(0, n_chunks, make_keys, 0)

    def value_bit(i, kth):
        cand = kth + (jnp.int32(1) << (31 - i))
        cnt = chunk_sum(lambda k0: _ones_where(key_ref[:, pl.ds(k0, kc)] >= cand))
        return jnp.where(cnt >= top_k, cand, kth)

    kth = lax.fori_loop(0, 32, value_bit, jnp.full((rows, 1), -2 ** 31, jnp.int32))
    n_above = chunk_sum(lambda k0: _ones_where(key_ref[:, pl.ds(k0, kc)] > kth))
    n_tied = chunk_sum(lambda k0: _ones_where(key_ref[:, pl.ds(k0, kc)] == kth))
    need = top_k - n_above
    row_has_tie = jnp.where(n_tied > need, _ones_where(kth > _NEG_INF_KEY), 0.0)
    any_tie = jnp.max(row_has_tie) > 0.0
    idx_bits = (score_ref.shape[1] - 1).bit_length()

    def index_bit(i, last):
        cand = last + (jnp.int32(1) << (idx_bits - 1 - i))
        cnt = chunk_sum(lambda k0: jnp.where(key_ref[:, pl.ds(k0, kc)] == kth,
                                             _ones_where(k0 + _iota((1, kc), 1) < cand), 0.0))
        return jnp.where(cnt < need, cand, last)

    last0 = jnp.where(any_tie, 0, score_ref.shape[1]) + jnp.zeros((rows, 1), jnp.int32)
    last = lax.fori_loop(0, jnp.where(any_tie, idx_bits, 0), index_bit, last0)

    def write_bias(c, carry):
        k0 = pl.multiple_of(c * kc, kc)
        key = key_ref[:, pl.ds(k0, kc)]
        taken = jnp.where(key > kth, 1.0, jnp.where(key == kth, _ones_where(k0 + _iota((1, kc), 1) <= last), 0.0))
        valid = score_ref[:, pl.ds(k0, kc)] > NEG_INF
        score_ref[:, pl.ds(k0, kc)] = jnp.where(valid, jnp.where(taken > 0.5, 0.0, NEG_INF), NEG_INF)
        return carry

    lax.fori_loop(0, n_chunks, write_bias, 0)


def _indexer_scores(logits, wi, rows, visible):
    logits = jnp.maximum(logits, 0.0)
    w = wi * (IDX_HEADS ** -0.5)
    score = logits[0:rows] * w[:, 0:1]
    for e in range(1, IDX_HEADS):
        score = score + logits[e * rows:(e + 1) * rows] * w[:, e:e + 1]
    return jnp.where(visible, score, NEG_INF)


def _stack_idx_heads(qi):
    return jnp.concatenate([qi[:, e * IDX_DIM:(e + 1) * IDX_DIM] for e in range(IDX_HEADS)], axis=0)


def _dsa_prompt_kernel(q_ref, qi_ref, wi_ref, ki_ref, kv_ref, o_ref, score_ref, key_ref, *, tq, kc, seq, n_heads):
    rep = n_heads // N_KV
    half = kv_ref.shape[1] // 2
    q0 = pl.program_id(1) * tq
    t = q0 + _iota((tq, 1), 0)
    n_chunks = (q0 + tq + kc - 1) // kc
    top_k = min(DSA_TOP_K, seq // 4)
    qs = _stack_idx_heads(qi_ref[...])
    wi = wi_ref[...]

    def score_chunk(c, carry):
        k0 = pl.multiple_of(c * kc, kc)
        kpos = k0 + _iota((1, kc), 1)
        logits = _dot_nt(qs, ki_ref[pl.ds(k0, kc), 0:IDX_DIM])
        score_ref[:, pl.ds(k0, kc)] = _indexer_scores(logits, wi, tq, kpos <= t)
        return carry

    lax.fori_loop(0, n_chunks, score_chunk, 0)
    _topk_bias(score_ref, key_ref, top_k, n_chunks, kc)
    q = q_ref[...]
    outs = []
    for g in range(N_KV):
        qg = _stack_heads(q, g, rep)

        def body(c, carry, qg=qg, g=g):
            k0 = pl.multiple_of(c * kc, kc)
            kk = kv_ref[pl.ds(k0, kc), g * HEAD_DIM:(g + 1) * HEAD_DIM]
            vv = kv_ref[pl.ds(k0, kc), half + g * HEAD_DIM:half + (g + 1) * HEAD_DIM]
            s = _dot_nt(qg, kk) + _rep_rows(score_ref[:, pl.ds(k0, kc)], rep)
            return _online_update(s, vv, *carry)

        init = (jnp.full((rep * tq, 1), M_INIT, F32), jnp.zeros((rep * tq, 1), F32),
                jnp.zeros((rep * tq, HEAD_DIM), F32))
        _, l_i, acc = lax.fori_loop(0, n_chunks, body, init)
        o_g = acc / _safe_den(l_i)
        outs += [o_g[r * tq:(r + 1) * tq] for r in range(rep)]
    o_ref[...] = jnp.concatenate(outs, axis=1)


def _dsa_prompt_attention(q_bf, qi_bf, wi, ki_bf, kv_bf, batch, seq, tq=128, kc=512):
    n, hd = q_bf.shape
    n_heads = hd // HEAD_DIM
    kvw = kv_bf.shape[1]
    nq = seq // tq
    assert seq % tq == 0 and seq % kc == 0
    row = lambda w: pl.BlockSpec((tq, w), lambda b, i: (b * nq + i, 0))
    return pl.pallas_call(
        functools.partial(_dsa_prompt_kernel, tq=tq, kc=kc, seq=seq, n_heads=n_heads),
        out_shape=jax.ShapeDtypeStruct((n, hd), F32),
        grid=(batch, nq),
        in_specs=[row(hd), row(qi_bf.shape[1]), row(LANES),
                  pl.BlockSpec((seq, LANES), lambda b, i: (b, 0)),
                  pl.BlockSpec((seq, kvw), lambda b, i: (b, 0))],
        out_specs=row(hd),
        scratch_shapes=[pltpu.VMEM((tq, seq), F32), pltpu.VMEM((tq, seq), jnp.int32)],
        compiler_params=_cparams(2),
        name="dsa_prompt_attention",
    )(q_bf, qi_bf, wi, ki_bf, kv_bf)


def _dsa_decode_score_kernel(pt_ref, qi_ref, wi_ref, kin_ref, *refs, n_pages):
    page_refs, o_ref = refs[:n_pages], refs[n_pages]
    rows = qi_ref.shape[1]
    k_t = jnp.concatenate([r[0, 0] for r in page_refs], axis=1).astype(BF16)
    past = k_t.shape[1]
    k_new = _pad_rows(kin_ref[0][:, 0:IDX_DIM], LANES)
    t = past + _iota((rows, 1), 0)
    qs = _stack_idx_heads(qi_ref[0])
    wi = wi_ref[0]
    o_ref[0] = jnp.concatenate(
        [_indexer_scores(_dot(qs, k_t), wi, rows, _iota((1, past), 1) <= t),
         _indexer_scores(_dot_nt(qs, k_new), wi, rows, past + _iota((1, LANES), 1) <= t)], axis=1)


def _dsa_decode_scores(qi_bf, wi, ki_new_bf, idx_t, layer, page_table):
    batch, rows, _ = qi_bf.shape
    _, _, kw, page = idx_t.shape
    n_pages = page_table.shape[1]
    s_pad = n_pages * page + LANES
    return pl.pallas_call(
        functools.partial(_dsa_decode_score_kernel, n_pages=n_pages),
        out_shape=jax.ShapeDtypeStruct((batch, rows, s_pad), F32),
        grid_spec=pltpu.PrefetchScalarGridSpec(
            num_scalar_prefetch=1, grid=(batch,),
            in_specs=[_per_seq(rows, qi_bf.shape[2]), _per_seq(rows, LANES), _per_seq(rows, LANES)]
            + _page_specs(n_pages, kw, page, layer),
            out_specs=_per_seq(rows, s_pad)),
        compiler_params=_cparams(1),
        name="dsa_decode_scores",
    )(page_table, qi_bf, wi, ki_new_bf, *([idx_t] * n_pages))


def _select_kernel(s_ref, o_ref, key_ref, *, top_k):
    o_ref[...] = s_ref[...]
    _topk_bias(o_ref, key_ref, top_k, 1, o_ref.shape[1])


def _topk_select(scores, top_k, tr=128):
    n, s = scores.shape
    tr = min(tr, n)
    assert n % tr == 0
    return pl.pallas_call(
        functools.partial(_select_kernel, top_k=top_k),
        out_shape=jax.ShapeDtypeStruct((n, s), F32),
        grid=(n // tr,),
        in_specs=[pl.BlockSpec((tr, s), lambda i: (i, 0))],
        out_specs=pl.BlockSpec((tr, s), lambda i: (i, 0)),
        scratch_shapes=[pltpu.VMEM((tr, s), jnp.int32)],
        compiler_params=_cparams(1),
        name="dsa_topk_select",
    )(scores)


def _dsa_decode_kernel(pt_ref, q_ref, bias_ref, kvn_ref, *refs, n_pages, page, n_heads):
    page_refs = refs[:n_pages]
    o_ref, kbuf = refs[n_pages:]
    rep = n_heads // N_KV
    kv_width, past = kbuf.shape
    half = kv_width // 2
    rows = q_ref.shape[1]
    for p in range(n_pages):
        kbuf[:, p * page:(p + 1) * page] = page_refs[p][0, 0].astype(BF16)
    kv_new = _pad_rows(kvn_ref[0], LANES)
    q = q_ref[0]
    bias = _rep_rows(bias_ref[0, :, 0:past], rep)
    bias_new = _rep_rows(bias_ref[0, :, past:past + LANES], rep)
    outs = []
    for g in range(N_KV):
        qg = _stack_heads(q, g, rep)
        k_rows = slice(g * HEAD_DIM, (g + 1) * HEAD_DIM)
        v_rows = slice(half + g * HEAD_DIM, half + (g + 1) * HEAD_DIM)
        p_a, p_b, den = _softmax_parts2(_dot(qg, kbuf[k_rows, :]) + bias, _dot_nt(qg, kv_new[:, k_rows]) + bias_new)
        o_g = (_dot_nt(p_a.astype(BF16), kbuf[v_rows, :]) + _dot(p_b.astype(BF16), kv_new[:, v_rows])) / _safe_den(den)
        outs += [o_g[r * rows:(r + 1) * rows] for r in range(rep)]
    o_ref[0] = jnp.concatenate(outs, axis=1)


def _dsa_decode_attention(q_bf, bias, kv_new_bf, kv_t, layer, page_table):
    batch, rows, hd = q_bf.shape
    n_heads = hd // HEAD_DIM
    _, _, kvw, page = kv_t.shape
    n_pages = page_table.shape[1]
    past = n_pages * page
    assert bias.shape[2] == past + LANES
    return pl.pallas_call(
        functools.partial(_dsa_decode_kernel, n_pages=n_pages, page=page, n_heads=n_heads),
        out_shape=jax.ShapeDtypeStruct((batch, rows, hd), F32),
        grid_spec=pltpu.PrefetchScalarGridSpec(
            num_scalar_prefetch=1, grid=(batch,),
            in_specs=[_per_seq(rows, hd), _per_seq(rows, past + LANES), _per_seq(rows, kvw)]
            + _page_specs(n_pages, kvw, page, layer),
            out_specs=_per_seq(rows, hd),
            scratch_shapes=[pltpu.VMEM((kvw, past), BF16)]),
        compiler_params=_cparams(1),
        name="dsa_decode_attention",
    )(page_table, q_bf, bias, kv_new_bf, *([kv_t] * n_pages))


def _pad_cols(w, width):
    return jnp.pad(w, ((0, 0), (0, width - w.shape[1])))


def _rope_tables(pos):
    half = HEAD_DIM // 2
    inv_freq = ROPE_THETA ** (-jnp.arange(half, dtype=F32) / half)
    ang = pos.astype(F32)[:, None] * inv_freq[None, :]
    cos, sin = jnp.cos(ang), jnp.sin(ang)
    reps = LANES // HEAD_DIM
    return (jnp.concatenate([cos, cos] * reps, axis=1), jnp.concatenate([-sin, sin] * reps, axis=1))


def _nsa_pack(w_in, hd, kvd, n_heads):
    gl0 = hd + 3 * kvd
    gate_cols = np.array([gl0 + h * 3 + c for c in range(3) for h in range(n_heads)])
    w = jnp.concatenate([w_in[:, :gl0], _pad_cols(w_in[:, gate_cols], LANES), w_in[:, gl0 + 3 * n_heads:]], axis=1)
    kv_outs = ((F32, 1.0), (BF16, 1.0))
    segs = [
        _Seg(0, hd, rope=hd, outs=((BF16, HEAD_DIM ** -0.5),)),
        _Seg(hd, kvd, rope=kvd // 2, outs=((F32, 1.0),)),
        _Seg(hd + kvd, kvd, rope=kvd // 2, outs=kv_outs),
        _Seg(hd + 2 * kvd, kvd, rope=kvd // 2, outs=kv_outs),
        _Seg(gl0, LANES),
        _Seg(gl0 + LANES, hd),
    ]
    return w.astype(BF16), segs


def _fox_pack(w_in, hd, n_heads):
    fl0 = 3 * hd
    w = jnp.concatenate([w_in[:, :fl0], _pad_cols(w_in[:, fl0:fl0 + n_heads], LANES), w_in[:, fl0 + n_heads:]], axis=1)
    segs = [
        _Seg(0, hd, outs=((BF16, HEAD_DIM ** -0.5),)),
        _Seg(hd, 2 * hd, outs=((F32, 1.0), (BF16, 1.0))),
        _Seg(fl0, LANES, logsig=True),
        _Seg(fl0 + LANES, hd),
    ]
    return w.astype(BF16), segs


def _dsa_pack(w_in, hd, kvd):
    qi0 = hd + kvd
    ki0 = qi0 + IDX_HEADS * IDX_DIM
    wi0 = ki0 + IDX_DIM
    z0 = wi0 + IDX_HEADS
    w = jnp.concatenate([w_in[:, :ki0], _pad_cols(w_in[:, ki0:wi0], LANES), _pad_cols(w_in[:, wi0:z0], LANES),
                         w_in[:, z0:]], axis=1)
    segs = [
        _Seg(0, hd, rope=hd, outs=((BF16, HEAD_DIM ** -0.5),)),
        _Seg(hd, kvd, rope=kvd // 2, outs=((F32, 1.0), (BF16, 1.0))),
        _Seg(qi0, IDX_HEADS * IDX_DIM, rope=IDX_HEADS * IDX_DIM, outs=((BF16, IDX_DIM ** -0.5),)),
        _Seg(ki0, LANES, rope=LANES, outs=((F32, 1.0), (BF16, 1.0))),
        _Seg(ki0 + LANES, LANES),
        _Seg(ki0 + 2 * LANES, hd),
    ]
    return w.astype(BF16), segs


def kernel(x_prompt, x_sample, cache_nsa_cmp_kv, cache_nsa_slc_kv, state_nsa_win_kv, cache_fox_kv, cache_fox_logf,
           cache_dsa_kv, cache_dsa_idx_k, page_table, norm_g, final_norm_g, nsa_w_in, nsa_w_out, fox_w_in, fox_b_f,
           fox_w_out, dsa_w_in, dsa_w_out):
    batch, seq, d_model = x_prompt.shape
    dec_batch, t_new, _ = x_sample.shape
    depth = norm_g.shape[0]
    hd = nsa_w_out.shape[1]
    n_heads = hd // HEAD_DIM
    kvd = 2 * N_KV * HEAD_DIM
    n_pages = page_table.shape[1]
    page = cache_fox_kv.shape[2]
    past = n_pages * page
    assert t_new <= ROWS_PAD and seq % NSA_BLOCK == 0
    tm = 256

    xp = x_prompt.reshape(batch * seq, d_model)
    xs = jnp.pad(x_sample, ((0, 0), (0, ROWS_PAD - t_new), (0, 0))).reshape(dec_batch * ROWS_PAD, d_model)
    cos_p, sin_p = _rope_tables(jnp.arange(seq, dtype=jnp.int32))
    tm_s = min(tm, dec_batch * ROWS_PAD)
    pos_s = past + (jnp.arange(tm_s, dtype=jnp.int32) % ROWS_PAD)
    cos_s, sin_s = _rope_tables(pos_s)
    zero_bias = jnp.zeros((1, LANES), F32)
    per_seq = lambda a: a.reshape(dec_batch, ROWS_PAD, a.shape[-1])
    trim = lambda a: per_seq(a)[:, :t_new]
    kv6 = lambda a, b: a.reshape(b, -1, 2, N_KV, HEAD_DIM)

    cmp_t, slc_t, win_t = _paged_view(cache_nsa_cmp_kv), _paged_view(cache_nsa_slc_kv), _paged_view(state_nsa_win_kv)
    fox_kv_t, fox_lf_t = _paged_view(cache_fox_kv), _paged_view(cache_fox_logf)
    dsa_kv_t, dsa_ik_t = _paged_view(cache_dsa_kv), _paged_view(cache_dsa_idx_k)

    outs = {k: [] for k in ("nsa_c_p", "nsa_c_s", "nsa_s_p", "nsa_s_s", "nsa_w_p", "nsa_w_new", "fox_kv_p", "fox_kv_s",
                            "fox_lf_p", "fox_lf_s", "dsa_kv_p", "dsa_kv_s", "dsa_ik_p", "dsa_ik_s")}
    yp_final = ys_final = None
    for i in range(depth):
        kind, j = i % N_MIXERS, i // N_MIXERS
        g = norm_g[i]
        last = i == depth - 1
        if kind == 0:
            w, segs = _nsa_pack(nsa_w_in[j], hd, kvd, n_heads)
            w_out = nsa_w_out[j].astype(BF16)
            q, kvc, kvs, kvs_b, kvw, kvw_b, gl, z = _project(xp, g, w, zero_bias, cos_p, sin_p, segs, tm)
            cmp_means = _blockmean_dense(kvc, batch, seq)
            o_p = _nsa_prompt_attention(q, gl, cmp_means, kvs_b, kvw_b, batch, seq)
            z_p = z
            outs["nsa_c_p"].append(kv6(kvc, batch))
            outs["nsa_s_p"].append(kv6(kvs, batch))
            outs["nsa_w_p"].append(kv6(kvw, batch)[:, seq - min(NSA_WINDOW, seq):])

            q, kvc, kvs, kvs_b, kvw, kvw_b, gl, z = _project(xs, g, w, zero_bias, cos_s, sin_s, segs, tm_s)
            cmp_means = _blockmean_paged(cmp_t, j, page_table, (past + t_new) // NSA_BLOCK)
            o_s = _nsa_decode_attention(per_seq(q), per_seq(gl), cmp_means, per_seq(kvs_b), per_seq(kvw_b), win_t,
                                        slc_t, j, page_table, t_new)
            o_s = o_s.reshape(dec_batch * ROWS_PAD, hd)
            z_s = z
            outs["nsa_c_s"].append(kv6(trim(kvc), dec_batch))
            outs["nsa_s_s"].append(kv6(trim(kvs), dec_batch))
            outs["nsa_w_new"].append(kv6(trim(kvw), dec_batch))
        elif kind == 1:
            w, segs = _fox_pack(fox_w_in[j], hd, n_heads)
            w_out = fox_w_out[j].astype(BF16)
            bias = _pad_cols(fox_b_f[j].reshape(1, n_heads), LANES)
            q, kv, kv_b, lf, z = _project(xp, g, w, bias, cos_p, sin_p, segs, tm)
            lf_p = lf[:, :n_heads].reshape(batch, seq, n_heads)
            o_p = _fox_prompt_attention(q, kv_b, _cumsum_dense(lf_p), batch, seq)
            z_p = z
            outs["fox_kv_p"].append(kv.reshape(batch, seq, 2, n_heads, HEAD_DIM))
            outs["fox_lf_p"].append(lf_p)

            q, kv, kv_b, lf, z = _project(xs, g, w, bias, cos_s, sin_s, segs, tm_s)
            c_keys, c_new, c_new_t = _cumsum_paged(fox_lf_t, j, page_table, per_seq(lf))
            o_s = _fox_decode_attention(per_seq(q), per_seq(kv_b), c_keys, c_new, c_new_t, fox_kv_t, j, page_table)
            o_s = o_s.reshape(dec_batch * ROWS_PAD, hd)
            z_s = z
            outs["fox_kv_s"].append(trim(kv).reshape(dec_batch, t_new, 2, n_heads, HEAD_DIM))
            outs["fox_lf_s"].append(trim(lf)[:, :, :n_heads])
        else:
            w, segs = _dsa_pack(dsa_w_in[j], hd, kvd)
            w_out = dsa_w_out[j].astype(BF16)
            q, kv, kv_b, qi, ki, ki_b, wi, z = _project(xp, g, w, zero_bias, cos_p, sin_p, segs, tm)
            o_p = _dsa_prompt_attention(q, qi, wi, ki_b, kv_b, batch, seq)
            z_p = z
            outs["dsa_kv_p"].append(kv.reshape(batch, seq, 2, N_KV, HEAD_DIM))
            outs["dsa_ik_p"].append(ki[:, :IDX_DIM].reshape(batch, seq, IDX_DIM))

            q, kv, kv_b, qi, ki, ki_b, wi, z = _project(xs, g, w, zero_bias, cos_s, sin_s, segs, tm_s)
            scores = _dsa_decode_scores(per_seq(qi), per_seq(wi), per_seq(ki_b), dsa_ik_t, j, page_table)
            s_pad = scores.shape[2]
            sel_bias = _topk_select(scores.reshape(dec_batch * ROWS_PAD, s_pad), min(DSA_TOP_K, (past + t_new) // 4))
            o_s = _dsa_decode_attention(per_seq(q), sel_bias.reshape(dec_batch, ROWS_PAD, s_pad), per_seq(kv_b),
                                        dsa_kv_t, j, page_table)
            o_s = o_s.reshape(dec_batch * ROWS_PAD, hd)
            z_s = z
            outs["dsa_kv_s"].append(trim(kv).reshape(dec_batch, t_new, 2, N_KV, HEAD_DIM))
            outs["dsa_ik_s"].append(trim(ki)[:, :, :IDX_DIM])
        fg = final_norm_g if last else None
        res_p = _out_project(o_p, z_p, xp, w_out, fg, tm)
        res_s = _out_project(o_s, z_s, xs, w_out, fg, tm_s)
        xp, xs = res_p[0], res_s[0]
        if last:
            yp_final, ys_final = res_p[1], res_s[1]

    y_prompt = yp_final.reshape(batch, seq, d_model)
    y_sample = ys_final.reshape(dec_batch, ROWS_PAD, d_model)[:, :t_new]
    st = jnp.stack
    keys_w = jnp.concatenate([state_nsa_win_kv, st(outs["nsa_w_new"])], axis=2)
    win_s = keys_w[:, :, keys_w.shape[2] - min(NSA_WINDOW, keys_w.shape[2]):]
    return (y_prompt, y_sample,
            st(outs["nsa_c_p"]), st(outs["nsa_c_s"]), st(outs["nsa_s_p"]), st(outs["nsa_s_s"]),
            st(outs["nsa_w_p"]), win_s,
            st(outs["fox_kv_p"]), st(outs["fox_kv_s"]), st(outs["fox_lf_p"]), st(outs["fox_lf_s"]),
            st(outs["dsa_kv_p"]), st(outs["dsa_kv_s"]), st(outs["dsa_ik_p"]), st(outs["dsa_ik_s"]))
```

```python
import functools

import numpy as np
import jax
import jax.numpy as jnp
from jax import lax
from jax.experimental import pallas as pl
from jax.experimental.pallas import tpu as pltpu

F32 = jnp.float32
BF16 = jnp.bfloat16
NEG_INF = float("-inf")
POS_INF = float("inf")

HEAD_DIM = 64
N_KV = 4
ROPE_THETA = 10000.0
RMS_EPS = 1e-6
NSA_BLOCK = 64
NSA_TOP_N = 16
NSA_WINDOW = 512
IDX_HEADS = 8
IDX_DIM = 64
DSA_TOP_K = 256
N_MIXERS = 3

LANES = 128
SUBLANES = 8
SUBLANES_BF16 = 16
VMEM_LIMIT = 56 * 1024 * 1024

ROWS_PAD = SUBLANES_BF16


def _cparams(n_axes):
    return pltpu.CompilerParams(dimension_semantics=("arbitrary",) * n_axes,
                                vmem_limit_bytes=VMEM_LIMIT)


def _dot(a, b):
    return jnp.dot(a, b, preferred_element_type=F32)


def _dot_nt(a, b):
    return lax.dot_general(a, b, (((1,), (1,)), ((), ())), preferred_element_type=F32)


def _iota(shape, dim):
    return lax.broadcasted_iota(jnp.int32, shape, dim)


def _log2(n):
    assert n & (n - 1) == 0
    return n.bit_length() - 1


def _ones_where(mask):
    return jnp.where(mask, 1.0, 0.0)


def _bias_where(*masks):
    bias = 0.0
    for mask in reversed(masks):
        bias = jnp.where(mask, bias, NEG_INF)
    return bias


def _softmax_parts(s):
    m = jnp.max(s, axis=-1, keepdims=True)
    m = jnp.where(m > NEG_INF, m, 0.0)
    p = jnp.exp(s - m)
    return p, jnp.sum(p, axis=-1, keepdims=True)


def _softmax_parts2(s_a, s_b):
    m = jnp.maximum(jnp.max(s_a, axis=-1, keepdims=True), jnp.max(s_b, axis=-1, keepdims=True))
    m = jnp.where(m > NEG_INF, m, 0.0)
    p_a = jnp.exp(s_a - m)
    p_b = jnp.exp(s_b - m)
    return p_a, p_b, jnp.sum(p_a, axis=-1, keepdims=True) + jnp.sum(p_b, axis=-1, keepdims=True)


def _safe_den(den):
    return jnp.where(den > 0, den, 1.0)


def _lane_fold(x, op=jnp.add):
    acc = x[:, 0:LANES]
    for b in range(1, x.shape[1] // LANES):
        acc = op(acc, x[:, b * LANES:(b + 1) * LANES])
    return acc


M_INIT = -1e30


def _online_update(s, vv, m_i, l_i, acc):
    m_new = jnp.maximum(m_i, jnp.max(s, axis=-1, keepdims=True))
    alpha = jnp.exp(m_i - m_new)
    p = jnp.exp(s - m_new)
    l_new = alpha * l_i + jnp.sum(p, axis=-1, keepdims=True)
    acc_new = alpha * acc + _dot(p.astype(BF16), vv)
    return m_new, l_new, acc_new


def _chunked_attention(n_chunks, n_streams, rows, score_fn, value_fn, scores_first):
    streams = range(n_streams)

    def body(c, carry):
        if scores_first:
            scores = [score_fn(c, j) for j in streams]
            return tuple(_online_update(scores[j], value_fn(c, j), *carry[j]) for j in streams)
        return tuple(_online_update(score_fn(c, j), value_fn(c, j), *carry[j]) for j in streams)

    init = (jnp.full((rows, 1), M_INIT, F32), jnp.zeros((rows, 1), F32), jnp.zeros((rows, HEAD_DIM), F32))
    state = lax.fori_loop(0, n_chunks, body, (init,) * n_streams)
    return [acc / _safe_den(l_i) for _, l_i, acc in state]


def _sigmoid(x):
    return 1.0 / (1.0 + jnp.exp(-x))


def _split3(x):
    hi = x.astype(BF16)
    r1 = x - hi.astype(F32)
    mid = r1.astype(BF16)
    lo = (r1 - mid.astype(F32)).astype(BF16)
    return hi, mid, lo


def _pad_rows(rows, total):
    return jnp.concatenate([rows, jnp.zeros((total - rows.shape[0], rows.shape[1]), rows.dtype)], axis=0)


class _Seg:
    def __init__(self, col, width, rope=0, logsig=False, outs=((F32, 1.0),)):
        self.col, self.width, self.rope, self.logsig, self.outs = col, width, rope, logsig, outs


def _rope(y, cos, sin):
    w = y.shape[1]
    reps = w // LANES
    cw = jnp.concatenate([cos] * reps, axis=1) if reps > 1 else cos
    sw = jnp.concatenate([sin] * reps, axis=1) if reps > 1 else sin
    half = HEAD_DIM // 2
    first = (_iota(y.shape, 1) & (HEAD_DIM - 1)) < half
    rot = jnp.where(first, pltpu.roll(y, w - half, 1), pltpu.roll(y, half, 1))
    return y * cw + rot * sw


def _log_sigmoid(x):
    return -(jnp.maximum(-x, 0.0) + jnp.log1p(jnp.exp(-jnp.abs(x))))


def _proj_kernel(x_ref, g_ref, w_ref, cos_ref, sin_ref, b_ref, *out_refs, segs):
    x = x_ref[...]
    h = x * lax.rsqrt(jnp.mean(x * x, axis=-1, keepdims=True) + RMS_EPS)
    hb = (h * g_ref[...]).astype(BF16)
    oi = 0
    for seg in segs:
        y = _dot(hb, w_ref[:, seg.col:seg.col + seg.width])
        if seg.rope == seg.width:
            y = _rope(y, cos_ref[...], sin_ref[...])
        elif seg.rope:
            y = jnp.concatenate([_rope(y[:, :seg.rope], cos_ref[...], sin_ref[...]), y[:, seg.rope:]], axis=1)
        if seg.logsig:
            y = _log_sigmoid(y + b_ref[...])
        for dtype, scale in seg.outs:
            out_refs[oi][...] = (y if scale == 1.0 else y * scale).astype(dtype)
            oi += 1


def _project(x, g, w_bf16, bias, cos, sin, segs, tm):
    n, d = x.shape
    assert n % tm == 0 and cos.shape[0] % tm == 0
    n_pos_tiles = cos.shape[0] // tm
    out_shape, out_specs = [], []
    for seg in segs:
        for dtype, _ in seg.outs:
            out_shape.append(jax.ShapeDtypeStruct((n, seg.width), dtype))
            out_specs.append(pl.BlockSpec((tm, seg.width), lambda i: (i, 0)))
    return pl.pallas_call(
        functools.partial(_proj_kernel, segs=segs),
        out_shape=out_shape,
        grid=(n // tm,),
        in_specs=[
            pl.BlockSpec((tm, d), lambda i: (i, 0)),
            pl.BlockSpec((1, d), lambda i: (0, 0)),
            pl.BlockSpec(w_bf16.shape, lambda i: (0, 0)),
            pl.BlockSpec((tm, LANES), lambda i: (i % n_pos_tiles, 0)),
            pl.BlockSpec((tm, LANES), lambda i: (i % n_pos_tiles, 0)),
            pl.BlockSpec((1, LANES), lambda i: (0, 0)),
        ],
        out_specs=out_specs,
        compiler_params=_cparams(1),
        name="rmsnorm_proj",
    )(x, g.reshape(1, d), w_bf16, cos, sin, bias)


def _out_kernel(o_ref, z_ref, x_ref, w_ref, g_ref, y_ref, *yn_ref):
    z = z_ref[...]
    a = o_ref[...] * (z * _sigmoid(z))
    y = x_ref[...] + _dot(a.astype(BF16), w_ref[...])
    y_ref[...] = y
    if yn_ref:
        yn_ref[0][...] = y * lax.rsqrt(jnp.mean(y * y, axis=-1, keepdims=True) + RMS_EPS) * g_ref[...]


def _out_project(o, z, x, w_bf16, final_g, tm):
    n, d = x.shape
    hd = o.shape[1]
    row = lambda w: pl.BlockSpec((tm, w), lambda i: (i, 0))
    final = final_g is not None
    g = final_g.reshape(1, d) if final else jnp.zeros((1, d), F32)
    out_shape = [jax.ShapeDtypeStruct((n, d), F32)] * (2 if final else 1)
    return pl.pallas_call(
        _out_kernel,
        out_shape=out_shape,
        grid=(n // tm,),
        in_specs=[row(hd), row(hd), row(d), pl.BlockSpec((hd, d), lambda i: (0, 0)),
                  pl.BlockSpec((1, d), lambda i: (0, 0))],
        out_specs=[row(d)] * (2 if final else 1),
        compiler_params=_cparams(1),
        name="gated_out_proj",
    )(o, z, x, w_bf16, g)


def _paged_view(cache):
    perm = (0, 1) + tuple(range(3, cache.ndim)) + (2,)
    return jnp.transpose(cache, perm).reshape(cache.shape[0], cache.shape[1], -1, cache.shape[2])


def _page_specs(n_pages, feat, page, layer):
    return [pl.BlockSpec((1, 1, feat, page), lambda b, pt, p=p: (layer, pt[b, p], 0, 0)) for p in range(n_pages)]


def _per_seq(*shape):
    return pl.BlockSpec((1,) + shape, lambda b, pt: (b,) + (0,) * len(shape))


def _blockmean_kernel(x_ref, o_ref, *, n_blocks):
    w = x_ref.shape[1]
    means = jnp.sum(x_ref[...].reshape(n_blocks, NSA_BLOCK, w), axis=1) * (1.0 / NSA_BLOCK)
    o_ref[0] = _pad_rows(means, LANES)


def _blockmean_dense(kvc, batch, seq):
    w = kvc.shape[1]
    n_blocks = seq // NSA_BLOCK
    assert n_blocks % SUBLANES == 0 and n_blocks <= LANES
    return pl.pallas_call(
        functools.partial(_blockmean_kernel, n_blocks=n_blocks),
        out_shape=jax.ShapeDtypeStruct((batch, LANES, w), F32),
        grid=(batch,),
        in_specs=[pl.BlockSpec((seq, w), lambda b: (b, 0))],
        out_specs=pl.BlockSpec((1, LANES, w), lambda b: (b, 0, 0)),
        compiler_params=_cparams(1),
        name="nsa_blockmean",
    )(kvc)


def _blockmean_paged_kernel(pt_ref, *refs, n_pages, n_blocks):
    page_refs, o_ref = refs[:n_pages], refs[n_pages]
    rows_t = jnp.concatenate([r[0, 0] for r in page_refs], axis=1)
    past = rows_t.shape[1]
    n_pad = -(-n_blocks // SUBLANES_BF16) * SUBLANES_BF16
    blk = _iota((n_pad, past), 0)
    member = jnp.where((_iota((n_pad, past), 1) >> _log2(NSA_BLOCK)) == blk, _ones_where(blk < n_blocks), 0.0)
    member = member.astype(BF16)
    hi, mid, lo = _split3(rows_t)
    sums = _dot_nt(member, hi) + _dot_nt(member, mid) + _dot_nt(member, lo)
    o_ref[0] = _pad_rows(sums * (1.0 / NSA_BLOCK), LANES)


def _blockmean_paged(pool_t, layer, page_table, n_blocks):
    _, _, w, page = pool_t.shape
    batch, n_pages = page_table.shape
    assert n_blocks * NSA_BLOCK <= n_pages * page and n_blocks <= LANES
    return pl.pallas_call(
        functools.partial(_blockmean_paged_kernel, n_pages=n_pages, n_blocks=n_blocks),
        out_shape=jax.ShapeDtypeStruct((batch, LANES, w), F32),
        grid_spec=pltpu.PrefetchScalarGridSpec(
            num_scalar_prefetch=1, grid=(batch,),
            in_specs=_page_specs(n_pages, w, page, layer),
            out_specs=_per_seq(LANES, w)),
        compiler_params=_cparams(1),
        name="nsa_blockmean_paged",
    )(page_table, *([pool_t] * n_pages))


def _stack_heads(q, g, rep):
    return jnp.concatenate(
        [q[:, (g * rep + r) * HEAD_DIM:(g * rep + r + 1) * HEAD_DIM] for r in range(rep)], axis=0)


def _rep_rows(a, rep):
    return jnp.concatenate([a] * rep, axis=0)


def _nsa_compressed(qg, cmp_bf, g, t, n_cmp, rows, rep, kv_width):
    half = kv_width // 2
    ck = cmp_bf[:, g * HEAD_DIM:(g + 1) * HEAD_DIM]
    cv = cmp_bf[:, half + g * HEAD_DIM:half + (g + 1) * HEAD_DIM]
    blk = _iota((1, LANES), 1)
    visible = _bias_where(blk < n_cmp, (blk + 1) * NSA_BLOCK - 1 <= t)
    s = _dot_nt(qg, ck) + _rep_rows(visible, rep)
    p, den = _softmax_parts(s)
    p = p / _safe_den(den)
    o_cmp = _dot(p.astype(BF16), cv)
    imp = p[0:rows]
    for r in range(1, rep):
        imp = imp + p[r * rows:(r + 1) * rows]
    return o_cmp, imp


def _nsa_select(imps, t0, rows, n_sel):
    n_grp = len(imps)
    n_pad = -(-n_sel // SUBLANES) * SUBLANES
    width = n_grp * LANES
    imp_t = jnp.concatenate(
        [(imp if rows == LANES else _pad_rows(imp, LANES)).T[0:n_pad, :] for imp in imps], axis=1)
    cur = (t0 + (_iota((1, width), 1) & (LANES - 1))) >> _log2(NSA_BLOCK)
    blk = _iota((n_pad, width), 0)
    score = jnp.where(blk == cur, POS_INF, jnp.where(blk == 0, POS_INF, jnp.where(blk < cur, imp_t, NEG_INF)))
    score = jnp.where(blk < n_sel, score, NEG_INF)
    tiles = [score[SUBLANES * r:SUBLANES * (r + 1)] for r in range(n_pad // SUBLANES)]
    ranks = [jnp.zeros((SUBLANES, width), F32) for _ in tiles]
    sub = _iota((SUBLANES, width), 0)
    for m in range(n_sel):
        sm = tiles[m // SUBLANES][m % SUBLANES:m % SUBLANES + 1, :]
        for r, tile in enumerate(tiles):
            if SUBLANES * r > m:
                beats = _ones_where(sm >= tile)
            elif SUBLANES * (r + 1) <= m:
                beats = _ones_where(sm > tile)
            else:
                loses_tie = _ones_where(sub > m - SUBLANES * r)
                beats = jnp.where(sm > tile, 1.0, jnp.where(sm == tile, loses_tie, 0.0))
            ranks[r] = ranks[r] + beats
    top_n = min(NSA_TOP_N, n_sel)
    sel_t = jnp.concatenate(
        [jnp.where(rank < top_n, _ones_where(tile > NEG_INF), 0.0) for rank, tile in zip(ranks, tiles)]
        + [jnp.zeros((LANES - n_pad, width), F32)], axis=0)
    return [sel_t[:, g * LANES:(g + 1) * LANES].T[0:rows, :] for g in range(n_grp)]


def _expand_blocks(sel, kpos):
    onehot = (kpos >> _log2(NSA_BLOCK)) == _iota((LANES, kpos.shape[1]), 0)
    return _dot(sel.astype(BF16), _ones_where(onehot).astype(BF16))


def _nsa_combine(gates, g, rep, rows, n_heads, o_cmp, o_sel, o_win):
    outs = []
    for r in range(rep):
        h = g * rep + r
        sl = slice(r * rows, (r + 1) * rows)
        outs.append(gates[:, h:h + 1] * o_cmp[sl]
                    + gates[:, n_heads + h:n_heads + h + 1] * o_sel[sl]
                    + gates[:, 2 * n_heads + h:2 * n_heads + h + 1] * o_win[sl])
    return outs


def _nsa_prompt_kernel(q_ref, gl_ref, cmp_ref, ks_ref, kw_ref, o_ref, *, tq, kc, seq, n_heads):
    rep = n_heads // N_KV
    kv_width = ks_ref.shape[1]
    half = kv_width // 2
    q0 = pl.program_id(1) * tq
    t = q0 + _iota((tq, 1), 0)
    n_blk = seq // NSA_BLOCK
    q = q_ref[...]
    gates = _sigmoid(gl_ref[...])
    cmp_bf = cmp_ref[0].astype(BF16)
    n_chunks = (q0 + tq + kc - 1) // kc
    win_lo = pl.multiple_of(jnp.maximum(q0 - NSA_WINDOW, 0), tq)
    win_len = NSA_WINDOW + tq
    wpos = win_lo + _iota((1, win_len), 1)
    in_window = _rep_rows(_bias_where(wpos <= t, wpos > t - NSA_WINDOW), rep)
    qgs = [_stack_heads(q, g, rep) for g in range(N_KV)]
    cmp_out = [_nsa_compressed(qgs[g], cmp_bf, g, t, n_blk, tq, rep, kv_width) for g in range(N_KV)]
    sels = _nsa_select([imp for _, imp in cmp_out], q0, tq, n_blk)

    def sel_scores(c, g):
        k0 = pl.multiple_of(c * kc, kc)
        kpos = k0 + _iota((1, kc), 1)
        allowed = _bias_where(_expand_blocks(sels[g], kpos) > 0.5, kpos <= t)
        return _dot_nt(qgs[g], ks_ref[pl.ds(k0, kc), g * HEAD_DIM:(g + 1) * HEAD_DIM]) + _rep_rows(allowed, rep)

    def sel_values(c, g):
        return ks_ref[pl.ds(pl.multiple_of(c * kc, kc), kc), half + g * HEAD_DIM:half + (g + 1) * HEAD_DIM]

    o_sels = _chunked_attention(n_chunks, N_KV, rep * tq, sel_scores, sel_values, scores_first=False)
    outs = []
    for g in range(N_KV):
        qg, o_cmp, o_sel = qgs[g], cmp_out[g][0], o_sels[g]

        kk = kw_ref[pl.ds(win_lo, win_len), g * HEAD_DIM:(g + 1) * HEAD_DIM]
        vv = kw_ref[pl.ds(win_lo, win_len), half + g * HEAD_DIM:half + (g + 1) * HEAD_DIM]
        p, den = _softmax_parts(_dot_nt(qg, kk) + in_window)
        o_win = _dot(p.astype(BF16), vv) / _safe_den(den)
        outs += _nsa_combine(gates, g, rep, tq, n_heads, o_cmp, o_sel, o_win)
    o_ref[...] = jnp.concatenate(outs, axis=1)


def _nsa_prompt_attention(q_bf, gl, cmp_means, ks_bf, kw_bf, batch, seq, kc=512):
    n, hd = q_bf.shape
    n_heads = hd // HEAD_DIM
    kvw = ks_bf.shape[1]
    tq = LANES
    assert seq % kc == 0 and seq % tq == 0 and NSA_WINDOW % tq == 0 and NSA_WINDOW + tq <= seq
    assert seq // NSA_BLOCK <= LANES and seq % NSA_BLOCK == 0
    nq = seq // tq
    return pl.pallas_call(
        functools.partial(_nsa_prompt_kernel, tq=tq, kc=kc, seq=seq, n_heads=n_heads),
        out_shape=jax.ShapeDtypeStruct((n, hd), F32),
        grid=(batch, nq),
        in_specs=[
            pl.BlockSpec((tq, hd), lambda b, i: (b * nq + i, 0)),
            pl.BlockSpec((tq, LANES), lambda b, i: (b * nq + i, 0)),
            pl.BlockSpec((1, LANES, kvw), lambda b, i: (b, 0, 0)),
            pl.BlockSpec((seq, kvw), lambda b, i: (b, 0)),
            pl.BlockSpec((seq, kvw), lambda b, i: (b, 0)),
        ],
        out_specs=pl.BlockSpec((tq, hd), lambda b, i: (b * nq + i, 0)),
        compiler_params=_cparams(2),
        name="nsa_prompt_attention",
    )(q_bf, gl, cmp_means, ks_bf, kw_bf)


def _nsa_decode_kernel(pt_ref, q_ref, gl_ref, cmp_ref, ksn_ref, kwn_ref, win_ref, *refs,
                       n_pages, page, t_new, n_heads):
    page_refs = refs[:n_pages]
    o_ref, kbuf, wbuf = refs[n_pages:]
    rep = n_heads // N_KV
    kv_width, past = kbuf.shape
    half = kv_width // 2
    rows = q_ref.shape[1]
    n_cmp = (past + t_new) // NSA_BLOCK
    n_sel = -(-(past + t_new) // NSA_BLOCK)
    w_state = wbuf.shape[1]
    for p in range(n_pages):
        kbuf[:, p * page:(p + 1) * page] = page_refs[p][0, 0].astype(BF16)
    wbuf[...] = win_ref[0, 0].astype(BF16)
    ks_new = _pad_rows(ksn_ref[0], LANES)
    kw_new = _pad_rows(kwn_ref[0], LANES)

    t = past + _iota((rows, 1), 0)
    q = q_ref[0]
    gates = _sigmoid(gl_ref[0])
    cmp_bf = cmp_ref[0].astype(BF16)
    kpos = _iota((1, past), 1)
    npos = past + _iota((1, LANES), 1)
    kpos_w = (past - w_state) + _iota((1, w_state), 1)
    okw = _rep_rows(_bias_where(kpos_w <= t, kpos_w > t - NSA_WINDOW, kpos_w >= 0), rep)
    okw_new = _rep_rows(_bias_where(npos <= t, npos > t - NSA_WINDOW), rep)
    qgs = [_stack_heads(q, g, rep) for g in range(N_KV)]
    cmp_out = [_nsa_compressed(qgs[g], cmp_bf, g, t, n_cmp, rows, rep, kv_width) for g in range(N_KV)]
    sels = _nsa_select([imp for _, imp in cmp_out], past, rows, n_sel)
    outs = []
    for g in range(N_KV):
        qg, sel, o_cmp = qgs[g], sels[g], cmp_out[g][0]
        k_rows = slice(g * HEAD_DIM, (g + 1) * HEAD_DIM)
        v_rows = slice(half + g * HEAD_DIM, half + (g + 1) * HEAD_DIM)
        ok = _rep_rows(_bias_where(_expand_blocks(sel, kpos) > 0.5, kpos <= t), rep)
        ok_new = _rep_rows(_bias_where(_expand_blocks(sel, npos) > 0.5, npos <= t), rep)
        p_a, p_b, den = _softmax_parts2(_dot(qg, kbuf[k_rows, :]) + ok, _dot_nt(qg, ks_new[:, k_rows]) + ok_new)
        o_sel = (_dot_nt(p_a.astype(BF16), kbuf[v_rows, :]) + _dot(p_b.astype(BF16), ks_new[:, v_rows])) / _safe_den(den)
        p_a, p_b, den = _softmax_parts2(_dot(qg, wbuf[k_rows, :]) + okw, _dot_nt(qg, kw_new[:, k_rows]) + okw_new)
        o_win = (_dot_nt(p_a.astype(BF16), wbuf[v_rows, :]) + _dot(p_b.astype(BF16), kw_new[:, v_rows])) / _safe_den(den)
        outs += _nsa_combine(gates, g, rep, rows, n_heads, o_cmp, o_sel, o_win)
    o_ref[0] = jnp.concatenate(outs, axis=1)


def _nsa_decode_attention(q_bf, gl, cmp_means, ks_new, kw_new, win_t, slc_t, layer, page_table, t_new):
    batch, rows, hd = q_bf.shape
    n_heads = hd // HEAD_DIM
    _, _, kvw, page = slc_t.shape
    n_pages = page_table.shape[1]
    past = n_pages * page
    w_state = win_t.shape[3]
    assert past % NSA_BLOCK == 0 and -(-(past + t_new) // NSA_BLOCK) <= LANES and rows <= LANES
    return pl.pallas_call(
        functools.partial(_nsa_decode_kernel, n_pages=n_pages, page=page, t_new=t_new, n_heads=n_heads),
        out_shape=jax.ShapeDtypeStruct((batch, rows, hd), F32),
        grid_spec=pltpu.PrefetchScalarGridSpec(
            num_scalar_prefetch=1, grid=(batch,),
            in_specs=[_per_seq(rows, hd), _per_seq(rows, LANES), _per_seq(LANES, kvw), _per_seq(rows, kvw),
                      _per_seq(rows, kvw),
                      pl.BlockSpec((1, 1, kvw, w_state), lambda b, pt: (layer, b, 0, 0))]
            + _page_specs(n_pages, kvw, page, layer),
            out_specs=_per_seq(rows, hd),
            scratch_shapes=[pltpu.VMEM((kvw, past), BF16), pltpu.VMEM((kvw, w_state), BF16)]),
        compiler_params=_cparams(1),
        name="nsa_decode_attention",
    )(page_table, q_bf, gl, cmp_means, ks_new, kw_new, win_t, *([slc_t] * n_pages))


def _cumsum_rows(x, tri_bf):
    hi, mid, lo = _split3(x)
    return _dot(tri_bf, hi) + _dot(tri_bf, mid) + _dot(tri_bf, lo)


def _cumsum_lanes(x, tri_bf):
    hi, mid, lo = _split3(x)
    return _dot(hi, tri_bf) + _dot(mid, tri_bf) + _dot(lo, tri_bf)


def _lower_tri(n):
    return _ones_where(_iota((n, n), 0) >= _iota((n, n), 1)).astype(BF16)


def _upper_tri(n):
    return _ones_where(_iota((n, n), 0) <= _iota((n, n), 1)).astype(BF16)


def _cumsum_kernel(x_ref, o_ref, *, chunk):
    n = x_ref.shape[1]
    tri = _lower_tri(chunk)
    carry = jnp.zeros((1, x_ref.shape[2]), F32)
    for c in range(n // chunk):
        cs = _cumsum_rows(x_ref[0, c * chunk:(c + 1) * chunk, :], tri) + carry
        o_ref[0, c * chunk:(c + 1) * chunk, :] = cs
        carry = cs[chunk - 1:chunk, :]


def _cumsum_dense(lf, chunk=256):
    batch, seq, h = lf.shape
    assert seq % chunk == 0
    return pl.pallas_call(
        functools.partial(_cumsum_kernel, chunk=chunk),
        out_shape=jax.ShapeDtypeStruct(lf.shape, F32),
        grid=(batch,),
        in_specs=[pl.BlockSpec((1, seq, h), lambda b: (b, 0, 0))],
        out_specs=pl.BlockSpec((1, seq, h), lambda b: (b, 0, 0)),
        compiler_params=_cparams(1),
        name="fox_cumsum",
    )(lf)


def _cumsum_paged_kernel(pt_ref, new_ref, *refs, n_pages, page):
    page_refs = refs[:n_pages]
    ck_ref, cn_ref, cnt_ref = refs[n_pages:]
    n_heads = ck_ref.shape[1]
    tri_u = _upper_tri(page)
    carry = jnp.zeros((n_heads, 1), F32)
    for p in range(n_pages):
        cs = _cumsum_lanes(page_refs[p][0, 0], tri_u) + carry
        ck_ref[0, :, p * page:(p + 1) * page] = cs
        carry = cs[:, page - 1:page]
    diag = _iota((n_heads, LANES), 0) == _iota((n_heads, LANES), 1)
    total = jnp.sum(jnp.where(diag, carry, 0.0), axis=0, keepdims=True)
    c_new = _cumsum_rows(_pad_rows(new_ref[0], LANES), _lower_tri(LANES)) + total
    cn_ref[0] = c_new[0:cn_ref.shape[1], :]
    cnt_ref[0] = c_new.T[0:n_heads, :]


def _cumsum_paged(logf_t, layer, page_table, lf_new):
    _, _, h, page = logf_t.shape
    batch, n_pages = page_table.shape
    rows = lf_new.shape[1]
    assert h <= LANES and rows <= LANES
    return pl.pallas_call(
        functools.partial(_cumsum_paged_kernel, n_pages=n_pages, page=page),
        out_shape=[jax.ShapeDtypeStruct((batch, h, n_pages * page), F32),
                   jax.ShapeDtypeStruct((batch, rows, LANES), F32),
                   jax.ShapeDtypeStruct((batch, h, LANES), F32)],
        grid_spec=pltpu.PrefetchScalarGridSpec(
            num_scalar_prefetch=1, grid=(batch,),
            in_specs=[_per_seq(rows, LANES)] + _page_specs(n_pages, h, page, layer),
            out_specs=[_per_seq(h, n_pages * page), _per_seq(rows, LANES), _per_seq(h, LANES)]),
        compiler_params=_cparams(1),
        name="fox_cumsum_paged",
    )(page_table, lf_new, *([logf_t] * n_pages))


def _fox_prompt_kernel(q_ref, k_ref, v_ref, cq_ref, ck_ref, o_ref, *, tq, kc):
    heads = q_ref.shape[1] // HEAD_DIM
    q0 = pl.program_id(2) * tq
    t = q0 + _iota((tq, 1), 0)
    n_chunks = (q0 + tq + kc - 1) // kc
    q = q_ref[...]
    cq = cq_ref[0, 0]

    def scores(c, j):
        k0 = pl.multiple_of(c * kc, kc)
        cols = slice(j * HEAD_DIM, (j + 1) * HEAD_DIM)
        decay = cq[:, j:j + 1] - ck_ref[0, 0, j:j + 1, pl.ds(k0, kc)]
        return _dot_nt(q[:, cols], k_ref[pl.ds(k0, kc), cols]) + jnp.where(k0 + _iota((1, kc), 1) <= t, decay, NEG_INF)

    def values(c, j):
        return v_ref[pl.ds(pl.multiple_of(c * kc, kc), kc), j * HEAD_DIM:(j + 1) * HEAD_DIM]

    o_ref[...] = jnp.concatenate(_chunked_attention(n_chunks, heads, tq, scores, values, scores_first=True), axis=1)


def _fox_prompt_attention(q_bf, kv_bf, csum, batch, seq, tq=256, kc=512, heads_per_step=4):
    n, hd = q_bf.shape
    wb = heads_per_step * HEAD_DIM
    n_hb = hd // wb
    nq = seq // tq
    assert seq % tq == 0 and seq % kc == 0 and wb % LANES == 0 and hd % wb == 0
    c_blocks = csum.reshape(batch, seq, n_hb, heads_per_step)
    cq = jnp.transpose(c_blocks, (0, 2, 1, 3))
    ck = jnp.transpose(c_blocks, (0, 2, 3, 1))
    return pl.pallas_call(
        functools.partial(_fox_prompt_kernel, tq=tq, kc=kc),
        out_shape=jax.ShapeDtypeStruct((n, hd), F32),
        grid=(batch, n_hb, nq),
        in_specs=[
            pl.BlockSpec((tq, wb), lambda b, h, i: (b * nq + i, h)),
            pl.BlockSpec((seq, wb), lambda b, h, i: (b, h)),
            pl.BlockSpec((seq, wb), lambda b, h, i: (b, n_hb + h)),
            pl.BlockSpec((1, 1, tq, heads_per_step), lambda b, h, i: (b, h, i, 0)),
            pl.BlockSpec((1, 1, heads_per_step, seq), lambda b, h, i: (b, h, 0, 0)),
        ],
        out_specs=pl.BlockSpec((tq, wb), lambda b, h, i: (b * nq + i, h)),
        compiler_params=_cparams(3),
        name="fox_prompt_attention",
    )(q_bf, kv_bf, kv_bf, cq, ck)


def _fox_decode_kernel(pt_ref, q_ref, kvn_ref, cq_ref, cnt_ref, ck_ref, *refs, n_pages, page, n_heads):
    page_refs = refs[:n_pages]
    o_ref, kbuf = refs[n_pages:]
    past = kbuf.shape[1]
    hd = n_heads * HEAD_DIM
    rows = q_ref.shape[1]
    for p in range(n_pages):
        kbuf[:, p * page:(p + 1) * page] = page_refs[p][0, 0].astype(BF16)
    kv_new = _pad_rows(kvn_ref[0], LANES)
    t = past + _iota((rows, 1), 0)
    visible = _iota((1, past), 1) <= t
    visible_new = past + _iota((1, LANES), 1) <= t
    q = q_ref[0]
    cq = cq_ref[0]
    outs = []
    for h in range(n_heads):
        k_rows = slice(h * HEAD_DIM, (h + 1) * HEAD_DIM)
        v_rows = slice(hd + h * HEAD_DIM, hd + (h + 1) * HEAD_DIM)
        qh = q[:, k_rows]
        s_a = _dot(qh, kbuf[k_rows, :]) + (cq[:, h:h + 1] - ck_ref[0, h:h + 1, :])
        s_b = _dot_nt(qh, kv_new[:, k_rows]) + (cq[:, h:h + 1] - cnt_ref[0, h:h + 1, :])
        p_a, p_b, den = _softmax_parts2(jnp.where(visible, s_a, NEG_INF), jnp.where(visible_new, s_b, NEG_INF))
        outs.append((_dot_nt(p_a.astype(BF16), kbuf[v_rows, :]) + _dot(p_b.astype(BF16), kv_new[:, v_rows]))
                    / _safe_den(den))
    o_ref[0] = jnp.concatenate(outs, axis=1)


def _fox_decode_attention(q_bf, kv_new_bf, c_keys, c_new, c_new_t, kv_t, layer, page_table):
    batch, rows, hd = q_bf.shape
    n_heads = hd // HEAD_DIM
    _, _, kvw, page = kv_t.shape
    n_pages = page_table.shape[1]
    past = n_pages * page
    return pl.pallas_call(
        functools.partial(_fox_decode_kernel, n_pages=n_pages, page=page, n_heads=n_heads),
        out_shape=jax.ShapeDtypeStruct((batch, rows, hd), F32),
        grid_spec=pltpu.PrefetchScalarGridSpec(
            num_scalar_prefetch=1, grid=(batch,),
            in_specs=[_per_seq(rows, hd), _per_seq(rows, kvw), _per_seq(rows, LANES), _per_seq(n_heads, LANES),
                      _per_seq(n_heads, past)] + _page_specs(n_pages, kvw, page, layer),
            out_specs=_per_seq(rows, hd),
            scratch_shapes=[pltpu.VMEM((kvw, past), BF16)]),
        compiler_params=_cparams(1),
        name="fox_decode_attention",
    )(page_table, q_bf, kv_new_bf, c_new, c_new_t, c_keys, *([kv_t] * n_pages))


def _order_key(bits):
    return bits ^ ((bits >> 31) & 0x7FFFFFFF)


_NEG_INF_KEY = int(_order_key(np.array(-np.inf, np.float32).view(np.int32)))


def _topk_bias(score_ref, key_ref, top_k, n_chunks, kc):
    rows = score_ref.shape[0]
    assert kc >= top_k and kc % LANES == 0

    def chunk_sum(fn):
        def body(c, acc):
            k0 = pl.multiple_of(c * kc, kc)
            return acc + _lane_fold(fn(k0))
        acc = lax.fori_loop(0, n_chunks, body, jnp.zeros((rows, LANES), F32))
        return jnp.sum(acc, axis=-1, keepdims=True)

    def make_keys(c, carry):
        k0 = pl.multiple_of(c * kc, kc)
        key_ref[:, pl.ds(k0, kc)] = _order_key(lax.bitcast_convert_type(score_ref[:, pl.ds(k0, kc)] + 0.0, jnp.int32))
        return carry

    lax.fori_loop(0, n_chunks, make_keys, 0)

    def value_bit(i, kth):
        cand = kth + (jnp.int32(1) << (31 - i))
        cnt = chunk_sum(lambda k0: _ones_where(key_ref[:, pl.ds(k0, kc)] >= cand))
        return jnp.where(cnt >= top_k, cand, kth)

    kth = lax.fori_loop(0, 32, value_bit, jnp.full((rows, 1), -2 ** 31, jnp.int32))
    n_above = chunk_sum(lambda k0: _ones_where(key_ref[:, pl.ds(k0, kc)] > kth))
    n_tied = chunk_sum(lambda k0: _ones_where(key_ref[:, pl.ds(k0, kc)] == kth))
    need = top_k - n_above
    row_has_tie = jnp.where(n_tied > need, _ones_where(kth > _NEG_INF_KEY), 0.0)
    any_tie = jnp.max(row_has_tie) > 0.0
    idx_bits = (score_ref.shape[1] - 1).bit_length()

    def index_bit(i, last):
        cand = last + (jnp.int32(1) << (idx_bits - 1 - i))
        cnt = chunk_sum(lambda k0: jnp.where(key_ref[:, pl.ds(k0, kc)] == kth,
                                             _ones_where(k0 + _iota((1, kc), 1) < cand), 0.0))
        return jnp.where(cnt < need, cand, last)

    last0 = jnp.where(any_tie, 0, score_ref.shape[1]) + jnp.zeros((rows, 1), jnp.int32)
    last = lax.fori_loop(0, jnp.where(any_tie, idx_bits, 0), index_bit, last0)

    def write_bias(c, carry):
        k0 = pl.multiple_of(c * kc, kc)
        key = key_ref[:, pl.ds(k0, kc)]
        taken = jnp.where(key > kth, 1.0, jnp.where(key == kth, _ones_where(k0 + _iota((1, kc), 1) <= last), 0.0))
        valid = score_ref[:, pl.ds(k0, kc)] > NEG_INF
        score_ref[:, pl.ds(k0, kc)] = jnp.where(valid, jnp.where(taken > 0.5, 0.0, NEG_INF), NEG_INF)
        return carry

    lax.fori_loop(0, n_chunks, write_bias, 0)


def _indexer_scores(logits, wi, rows, visible):
    logits = jnp.maximum(logits, 0.0)
    w = wi * (IDX_HEADS ** -0.5)
    score = logits[0:rows] * w[:, 0:1]
    for e in range(1, IDX_HEADS):
        score = score + logits[e * rows:(e + 1) * rows] * w[:, e:e + 1]
    return jnp.where(visible, score, NEG_INF)


def _stack_idx_heads(qi):
    return jnp.concatenate([qi[:, e * IDX_DIM:(e + 1) * IDX_DIM] for e in range(IDX_HEADS)], axis=0)


def _dsa_prompt_kernel(q_ref, qi_ref, wi_ref, ki_ref, kv_ref, o_ref, score_ref, key_ref, *, tq, kc, seq, n_heads):
    rep = n_heads // N_KV
    half = kv_ref.shape[1] // 2
    q0 = pl.program_id(1) * tq
    t = q0 + _iota((tq, 1), 0)
    n_chunks = (q0 + tq + kc - 1) // kc
    top_k = min(DSA_TOP_K, seq // 4)
    qs = _stack_idx_heads(qi_ref[...])
    wi = wi_ref[...]

    def score_chunk(c, carry):
        k0 = pl.multiple_of(c * kc, kc)
        kpos = k0 + _iota((1, kc), 1)
        logits = _dot_nt(qs, ki_ref[pl.ds(k0, kc), 0:IDX_DIM])
        score_ref[:, pl.ds(k0, kc)] = _indexer_scores(logits, wi, tq, kpos <= t)
        return carry

    lax.fori_loop(0, n_chunks, score_chunk, 0)
    _topk_bias(score_ref, key_ref, top_k, n_chunks, kc)
    q = q_ref[...]
    qgs = [_stack_heads(q, g, rep) for g in range(N_KV)]

    def scores(c, g):
        k0 = pl.multiple_of(c * kc, kc)
        return (_dot_nt(qgs[g], kv_ref[pl.ds(k0, kc), g * HEAD_DIM:(g + 1) * HEAD_DIM])
                + _rep_rows(score_ref[:, pl.ds(k0, kc)], rep))

    def values(c, g):
        return kv_ref[pl.ds(pl.multiple_of(c * kc, kc), kc), half + g * HEAD_DIM:half + (g + 1) * HEAD_DIM]

    outs = []
    for o_g in _chunked_attention(n_chunks, N_KV, rep * tq, scores, values, scores_first=False):
        outs += [o_g[r * tq:(r + 1) * tq] for r in range(rep)]
    o_ref[...] = jnp.concatenate(outs, axis=1)


def _dsa_prompt_attention(q_bf, qi_bf, wi, ki_bf, kv_bf, batch, seq, tq=256, kc=512):
    n, hd = q_bf.shape
    n_heads = hd // HEAD_DIM
    kvw = kv_bf.shape[1]
    nq = seq // tq
    assert seq % tq == 0 and seq % kc == 0
    row = lambda w: pl.BlockSpec((tq, w), lambda b, i: (b * nq + i, 0))
    return pl.pallas_call(
        functools.partial(_dsa_prompt_kernel, tq=tq, kc=kc, seq=seq, n_heads=n_heads),
        out_shape=jax.ShapeDtypeStruct((n, hd), F32),
        grid=(batch, nq),
        in_specs=[row(hd), row(qi_bf.shape[1]), row(LANES),
                  pl.BlockSpec((seq, LANES), lambda b, i: (b, 0)),
                  pl.BlockSpec((seq, kvw), lambda b, i: (b, 0))],
        out_specs=row(hd),
        scratch_shapes=[pltpu.VMEM((tq, seq), F32), pltpu.VMEM((tq, seq), jnp.int32)],
        compiler_params=_cparams(2),
        name="dsa_prompt_attention",
    )(q_bf, qi_bf, wi, ki_bf, kv_bf)


def _dsa_decode_score_kernel(pt_ref, qi_ref, wi_ref, kin_ref, *refs, n_pages):
    page_refs, o_ref = refs[:n_pages], refs[n_pages]
    rows = qi_ref.shape[1]
    k_t = jnp.concatenate([r[0, 0] for r in page_refs], axis=1).astype(BF16)
    past = k_t.shape[1]
    k_new = _pad_rows(kin_ref[0][:, 0:IDX_DIM], LANES)
    t = past + _iota((rows, 1), 0)
    qs = _stack_idx_heads(qi_ref[0])
    wi = wi_ref[0]
    o_ref[0] = jnp.concatenate(
        [_indexer_scores(_dot(qs, k_t), wi, rows, _iota((1, past), 1) <= t),
         _indexer_scores(_dot_nt(qs, k_new), wi, rows, past + _iota((1, LANES), 1) <= t)], axis=1)


def _dsa_decode_scores(qi_bf, wi, ki_new_bf, idx_t, layer, page_table):
    batch, rows, _ = qi_bf.shape
    _, _, kw, page = idx_t.shape
    n_pages = page_table.shape[1]
    s_pad = n_pages * page + LANES
    return pl.pallas_call(
        functools.partial(_dsa_decode_score_kernel, n_pages=n_pages),
        out_shape=jax.ShapeDtypeStruct((batch, rows, s_pad), F32),
        grid_spec=pltpu.PrefetchScalarGridSpec(
            num_scalar_prefetch=1, grid=(batch,),
            in_specs=[_per_seq(rows, qi_bf.shape[2]), _per_seq(rows, LANES), _per_seq(rows, LANES)]
            + _page_specs(n_pages, kw, page, layer),
            out_specs=_per_seq(rows, s_pad)),
        compiler_params=_cparams(1),
        name="dsa_decode_scores",
    )(page_table, qi_bf, wi, ki_new_bf, *([idx_t] * n_pages))


def _select_kernel(s_ref, o_ref, key_ref, *, top_k):
    o_ref[...] = s_ref[...]
    _topk_bias(o_ref, key_ref, top_k, 1, o_ref.shape[1])


def _topk_select(scores, top_k, tr=128):
    n, s = scores.shape
    tr = min(tr, n)
    assert n % tr == 0
    return pl.pallas_call(
        functools.partial(_select_kernel, top_k=top_k),
        out_shape=jax.ShapeDtypeStruct((n, s), F32),
        grid=(n // tr,),
        in_specs=[pl.BlockSpec((tr, s), lambda i: (i, 0))],
        out_specs=pl.BlockSpec((tr, s), lambda i: (i, 0)),
        scratch_shapes=[pltpu.VMEM((tr, s), jnp.int32)],
        compiler_params=_cparams(1),
        name="dsa_topk_select",
    )(scores)


def _dsa_decode_kernel(pt_ref, q_ref, bias_ref, kvn_ref, *refs, n_pages, page, n_heads):
    page_refs = refs[:n_pages]
    o_ref, kbuf = refs[n_pages:]
    rep = n_heads // N_KV
    kv_width, past = kbuf.shape
    half = kv_width // 2
    rows = q_ref.shape[1]
    for p in range(n_pages):
        kbuf[:, p * page:(p + 1) * page] = page_refs[p][0, 0].astype(BF16)
    kv_new = _pad_rows(kvn_ref[0], LANES)
    q = q_ref[0]
    bias = _rep_rows(bias_ref[0, :, 0:past], rep)
    bias_new = _rep_rows(bias_ref[0, :, past:past + LANES], rep)
    outs = []
    for g in range(N_KV):
        qg = _stack_heads(q, g, rep)
        k_rows = slice(g * HEAD_DIM, (g + 1) * HEAD_DIM)
        v_rows = slice(half + g * HEAD_DIM, half + (g + 1) * HEAD_DIM)
        p_a, p_b, den = _softmax_parts2(_dot(qg, kbuf[k_rows, :]) + bias, _dot_nt(qg, kv_new[:, k_rows]) + bias_new)
        o_g = (_dot_nt(p_a.astype(BF16), kbuf[v_rows, :]) + _dot(p_b.astype(BF16), kv_new[:, v_rows])) / _safe_den(den)
        outs += [o_g[r * rows:(r + 1) * rows] for r in range(rep)]
    o_ref[0] = jnp.concatenate(outs, axis=1)


def _dsa_decode_attention(q_bf, bias, kv_new_bf, kv_t, layer, page_table):
    batch, rows, hd = q_bf.shape
    n_heads = hd // HEAD_DIM
    _, _, kvw, page = kv_t.shape
    n_pages = page_table.shape[1]
    past = n_pages * page
    assert bias.shape[2] == past + LANES
    return pl.pallas_call(
        functools.partial(_dsa_decode_kernel, n_pages=n_pages, page=page, n_heads=n_heads),
        out_shape=jax.ShapeDtypeStruct((batch, rows, hd), F32),
        grid_spec=pltpu.PrefetchScalarGridSpec(
            num_scalar_prefetch=1, grid=(batch,),
            in_specs=[_per_seq(rows, hd), _per_seq(rows, past + LANES), _per_seq(rows, kvw)]
            + _page_specs(n_pages, kvw, page, layer),
            out_specs=_per_seq(rows, hd),
            scratch_shapes=[pltpu.VMEM((kvw, past), BF16)]),
        compiler_params=_cparams(1),
        name="dsa_decode_attention",
    )(page_table, q_bf, bias, kv_new_bf, *([kv_t] * n_pages))


def _pad_cols(w, width):
    return jnp.pad(w, ((0, 0), (0, width - w.shape[1])))


def _rope_tables(pos):
    half = HEAD_DIM // 2
    inv_freq = ROPE_THETA ** (-jnp.arange(half, dtype=F32) / half)
    ang = pos.astype(F32)[:, None] * inv_freq[None, :]
    cos, sin = jnp.cos(ang), jnp.sin(ang)
    reps = LANES // HEAD_DIM
    return (jnp.concatenate([cos, cos] * reps, axis=1), jnp.concatenate([-sin, sin] * reps, axis=1))


def _nsa_pack(w_in, hd, kvd, n_heads):
    gl0 = hd + 3 * kvd
    gate_cols = np.array([gl0 + h * 3 + c for c in range(3) for h in range(n_heads)])
    w = jnp.concatenate([w_in[:, :gl0], _pad_cols(w_in[:, gate_cols], LANES), w_in[:, gl0 + 3 * n_heads:]], axis=1)
    kv_outs = ((F32, 1.0), (BF16, 1.0))
    segs = [
        _Seg(0, hd, rope=hd, outs=((BF16, HEAD_DIM ** -0.5),)),
        _Seg(hd, kvd, rope=kvd // 2, outs=((F32, 1.0),)),
        _Seg(hd + kvd, kvd, rope=kvd // 2, outs=kv_outs),
        _Seg(hd + 2 * kvd, kvd, rope=kvd // 2, outs=kv_outs),
        _Seg(gl0, LANES),
        _Seg(gl0 + LANES, hd),
    ]
    return w.astype(BF16), segs


def _fox_pack(w_in, hd, n_heads):
    fl0 = 3 * hd
    w = jnp.concatenate([w_in[:, :fl0], _pad_cols(w_in[:, fl0:fl0 + n_heads], LANES), w_in[:, fl0 + n_heads:]], axis=1)
    segs = [
        _Seg(0, hd, outs=((BF16, HEAD_DIM ** -0.5),)),
        _Seg(hd, 2 * hd, outs=((F32, 1.0), (BF16, 1.0))),
        _Seg(fl0, LANES, logsig=True),
        _Seg(fl0 + LANES, hd),
    ]
    return w.astype(BF16), segs


def _dsa_pack(w_in, hd, kvd):
    qi0 = hd + kvd
    ki0 = qi0 + IDX_HEADS * IDX_DIM
    wi0 = ki0 + IDX_DIM
    z0 = wi0 + IDX_HEADS
    w = jnp.concatenate([w_in[:, :ki0], _pad_cols(w_in[:, ki0:wi0], LANES), _pad_cols(w_in[:, wi0:z0], LANES),
                         w_in[:, z0:]], axis=1)
    segs = [
        _Seg(0, hd, rope=hd, outs=((BF16, HEAD_DIM ** -0.5),)),
        _Seg(hd, kvd, rope=kvd // 2, outs=((F32, 1.0), (BF16, 1.0))),
        _Seg(qi0, IDX_HEADS * IDX_DIM, rope=IDX_HEADS * IDX_DIM, outs=((BF16, IDX_DIM ** -0.5),)),
        _Seg(ki0, LANES, rope=LANES, outs=((F32, 1.0), (BF16, 1.0))),
        _Seg(ki0 + LANES, LANES),
        _Seg(ki0 + 2 * LANES, hd),
    ]
    return w.astype(BF16), segs


def kernel(x_prompt, x_sample, cache_nsa_cmp_kv, cache_nsa_slc_kv, state_nsa_win_kv, cache_fox_kv, cache_fox_logf,
           cache_dsa_kv, cache_dsa_idx_k, page_table, norm_g, final_norm_g, nsa_w_in, nsa_w_out, fox_w_in, fox_b_f,
           fox_w_out, dsa_w_in, dsa_w_out):
    batch, seq, d_model = x_prompt.shape
    dec_batch, t_new, _ = x_sample.shape
    depth = norm_g.shape[0]
    hd = nsa_w_out.shape[1]
    n_heads = hd // HEAD_DIM
    kvd = 2 * N_KV * HEAD_DIM
    n_pages = page_table.shape[1]
    page = cache_fox_kv.shape[2]
    past = n_pages * page
    assert t_new <= ROWS_PAD and seq % NSA_BLOCK == 0
    tm = 256

    xp = x_prompt.reshape(batch * seq, d_model)
    xs = jnp.pad(x_sample, ((0, 0), (0, ROWS_PAD - t_new), (0, 0))).reshape(dec_batch * ROWS_PAD, d_model)
    cos_p, sin_p = _rope_tables(jnp.arange(seq, dtype=jnp.int32))
    tm_s = min(tm, dec_batch * ROWS_PAD)
    pos_s = past + (jnp.arange(tm_s, dtype=jnp.int32) % ROWS_PAD)
    cos_s, sin_s = _rope_tables(pos_s)
    zero_bias = jnp.zeros((1, LANES), F32)
    per_seq = lambda a: a.reshape(dec_batch, ROWS_PAD, a.shape[-1])
    trim = lambda a: per_seq(a)[:, :t_new]
    kv6 = lambda a, b: a.reshape(b, -1, 2, N_KV, HEAD_DIM)

    cmp_t, slc_t, win_t = _paged_view(cache_nsa_cmp_kv), _paged_view(cache_nsa_slc_kv), _paged_view(state_nsa_win_kv)
    fox_kv_t, fox_lf_t = _paged_view(cache_fox_kv), _paged_view(cache_fox_logf)
    dsa_kv_t, dsa_ik_t = _paged_view(cache_dsa_kv), _paged_view(cache_dsa_idx_k)

    outs = {k: [] for k in ("nsa_c_p", "nsa_c_s", "nsa_s_p", "nsa_s_s", "nsa_w_p", "nsa_w_new", "fox_kv_p", "fox_kv_s",
                            "fox_lf_p", "fox_lf_s", "dsa_kv_p", "dsa_kv_s", "dsa_ik_p", "dsa_ik_s")}
    yp_final = ys_final = None
    for i in range(depth):
        kind, j = i % N_MIXERS, i // N_MIXERS
        g = norm_g[i]
        last = i == depth - 1
        if kind == 0:
            w, segs = _nsa_pack(nsa_w_in[j], hd, kvd, n_heads)
            w_out = nsa_w_out[j].astype(BF16)
            q, kvc, kvs, kvs_b, kvw, kvw_b, gl, z = _project(xp, g, w, zero_bias, cos_p, sin_p, segs, tm)
            cmp_means = _blockmean_dense(kvc, batch, seq)
            o_p = _nsa_prompt_attention(q, gl, cmp_means, kvs_b, kvw_b, batch, seq)
            z_p = z
            outs["nsa_c_p"].append(kv6(kvc, batch))
            outs["nsa_s_p"].append(kv6(kvs, batch))
            outs["nsa_w_p"].append(kv6(kvw, batch)[:, seq - min(NSA_WINDOW, seq):])

            q, kvc, kvs, kvs_b, kvw, kvw_b, gl, z = _project(xs, g, w, zero_bias, cos_s, sin_s, segs, tm_s)
            cmp_means = _blockmean_paged(cmp_t, j, page_table, (past + t_new) // NSA_BLOCK)
            o_s = _nsa_decode_attention(per_seq(q), per_seq(gl), cmp_means, per_seq(kvs_b), per_seq(kvw_b), win_t,
                                        slc_t, j, page_table, t_new)
            o_s = o_s.reshape(dec_batch * ROWS_PAD, hd)
            z_s = z
            outs["nsa_c_s"].append(kv6(trim(kvc), dec_batch))
            outs["nsa_s_s"].append(kv6(trim(kvs), dec_batch))
            outs["nsa_w_new"].append(kv6(trim(kvw), dec_batch))
        elif kind == 1:
            w, segs = _fox_pack(fox_w_in[j], hd, n_heads)
            w_out = fox_w_out[j].astype(BF16)
            bias = _pad_cols(fox_b_f[j].reshape(1, n_heads), LANES)
            q, kv, kv_b, lf, z = _project(xp, g, w, bias, cos_p, sin_p, segs, tm)
            lf_p = lf[:, :n_heads].reshape(batch, seq, n_heads)
            o_p = _fox_prompt_attention(q, kv_b, _cumsum_dense(lf_p), batch, seq)
            z_p = z
            outs["fox_kv_p"].append(kv.reshape(batch, seq, 2, n_heads, HEAD_DIM))
            outs["fox_lf_p"].append(lf_p)

            q, kv, kv_b, lf, z = _project(xs, g, w, bias, cos_s, sin_s, segs, tm_s)
            c_keys, c_new, c_new_t = _cumsum_paged(fox_lf_t, j, page_table, per_seq(lf))
            o_s = _fox_decode_attention(per_seq(q), per_seq(kv_b), c_keys, c_new, c_new_t, fox_kv_t, j, page_table)
            o_s = o_s.reshape(dec_batch * ROWS_PAD, hd)
            z_s = z
            outs["fox_kv_s"].append(trim(kv).reshape(dec_batch, t_new, 2, n_heads, HEAD_DIM))
            outs["fox_lf_s"].append(trim(lf)[:, :, :n_heads])
        else:
            w, segs = _dsa_pack(dsa_w_in[j], hd, kvd)
            w_out = dsa_w_out[j].astype(BF16)
            q, kv, kv_b, qi, ki, ki_b, wi, z = _project(xp, g, w, zero_bias, cos_p, sin_p, segs, tm)
            o_p = _dsa_prompt_attention(q, qi, wi, ki_b, kv_b, batch, seq)
            z_p = z
            outs["dsa_kv_p"].append(kv.reshape(batch, seq, 2, N_KV, HEAD_DIM))
            outs["dsa_ik_p"].append(ki[:, :IDX_DIM].reshape(batch, seq, IDX_DIM))

            q, kv, kv_b, qi, ki, ki_b, wi, z = _project(xs, g, w, zero_bias, cos_s, sin_s, segs, tm_s)
            scores = _dsa_decode_scores(per_seq(qi), per_seq(wi), per_seq(ki_b), dsa_ik_t, j, page_table)
            s_pad = scores.shape[2]
            sel_bias = _topk_select(scores.reshape(dec_batch * ROWS_PAD, s_pad), min(DSA_TOP_K, (past + t_new) // 4))
            o_s = _dsa_decode_attention(per_seq(q), sel_bias.reshape(dec_batch, ROWS_PAD, s_pad), per_seq(kv_b),
                                        dsa_kv_t, j, page_table)
            o_s = o_s.reshape(dec_batch * ROWS_PAD, hd)
            z_s = z
            outs["dsa_kv_s"].append(trim(kv).reshape(dec_batch, t_new, 2, N_KV, HEAD_DIM))
            outs["dsa_ik_s"].append(trim(ki)[:, :, :IDX_DIM])
        fg = final_norm_g if last else None
        res_p = _out_project(o_p, z_p, xp, w_out, fg, tm)
        res_s = _out_project(o_s, z_s, xs, w_out, fg, tm_s)
        xp, xs = res_p[0], res_s[0]
        if last:
            yp_final, ys_final = res_p[1], res_s[1]

    y_prompt = yp_final.reshape(batch, seq, d_model)
    y_sample = ys_final.reshape(dec_batch, ROWS_PAD, d_model)[:, :t_new]
    st = jnp.stack
    keys_w = jnp.concatenate([state_nsa_win_kv, st(outs["nsa_w_new"])], axis=2)
    win_s = keys_w[:, :, keys_w.shape[2] - min(NSA_WINDOW, keys_w.shape[2]):]
    return (y_prompt, y_sample,
            st(outs["nsa_c_p"]), st(outs["nsa_c_s"]), st(outs["nsa_s_p"]), st(outs["nsa_s_s"]),
            st(outs["nsa_w_p"]), win_s,
            st(outs["fox_kv_p"]), st(outs["fox_kv_s"]), st(outs["fox_lf_p"]), st(outs["fox_lf_s"]),
            st(outs["dsa_kv_p"]), st(outs["dsa_kv_s"]), st(outs["dsa_ik_p"]), st(outs["dsa_ik_s"]))
```

```python
import functools

import numpy as np
import jax
import jax.numpy as jnp
from jax import lax
from jax.experimental import pallas as pl
from jax.experimental.pallas import tpu as pltpu

F32 = jnp.float32
BF16 = jnp.bfloat16
NEG_INF = float("-inf")
POS_INF = float("inf")

HEAD_DIM = 64
N_KV = 4
ROPE_THETA = 10000.0
RMS_EPS = 1e-6
NSA_BLOCK = 64
NSA_TOP_N = 16
NSA_WINDOW = 512
IDX_HEADS = 8
IDX_DIM = 64
DSA_TOP_K = 256
N_MIXERS = 3

LANES = 128
SUBLANES = 8
SUBLANES_BF16 = 16
VMEM_LIMIT = 56 * 1024 * 1024

ROWS_PAD = SUBLANES_BF16
VALUE_SLOT = LANES


def _cparams(n_axes):
    return pltpu.CompilerParams(dimension_semantics=("arbitrary",) * n_axes,
                                vmem_limit_bytes=VMEM_LIMIT)


def _dot(a, b):
    return jnp.dot(a, b, preferred_element_type=F32)


def _dot_nt(a, b):
    return lax.dot_general(a, b, (((1,), (1,)), ((), ())), preferred_element_type=F32)


def _iota(shape, dim):
    return lax.broadcasted_iota(jnp.int32, shape, dim)


def _log2(n):
    assert n & (n - 1) == 0
    return n.bit_length() - 1


def _ones_where(mask):
    return jnp.where(mask, 1.0, 0.0)


def _bias_where(*masks):
    bias = 0.0
    for mask in reversed(masks):
        bias = jnp.where(mask, bias, NEG_INF)
    return bias


def _softmax_parts(s):
    m = jnp.max(s, axis=-1, keepdims=True)
    m = jnp.where(m > NEG_INF, m, 0.0)
    p = jnp.exp(s - m)
    return p, jnp.sum(p, axis=-1, keepdims=True)


def _softmax_parts2(s_a, s_b):
    m = jnp.maximum(jnp.max(s_a, axis=-1, keepdims=True), jnp.max(s_b, axis=-1, keepdims=True))
    m = jnp.where(m > NEG_INF, m, 0.0)
    p_a = jnp.exp(s_a - m)
    p_b = jnp.exp(s_b - m)
    return p_a, p_b, jnp.sum(p_a, axis=-1, keepdims=True) + jnp.sum(p_b, axis=-1, keepdims=True)


def _safe_den(den):
    return jnp.where(den > 0, den, 1.0)


def _lane_fold(x, op=jnp.add):
    acc = x[:, 0:LANES]
    for b in range(1, x.shape[1] // LANES):
        acc = op(acc, x[:, b * LANES:(b + 1) * LANES])
    return acc


M_INIT = -1e30


def _online_update(s, v1, m_i, acc):
    m_new = jnp.maximum(m_i, jnp.max(s, axis=-1, keepdims=True))
    p = jnp.exp(s - m_new)
    return m_new, jnp.exp(m_i - m_new) * acc + _dot(p.astype(BF16), v1)


def _normalize(acc):
    return acc[:, 0:HEAD_DIM] / _safe_den(acc[:, HEAD_DIM:HEAD_DIM + 1])


def _chunked_attention(n_chunks, n_streams, rows, score_fn, value_fn, scores_first):
    streams = range(n_streams)

    def body(c, carry):
        if scores_first:
            scores = [score_fn(c, j) for j in streams]
            return tuple(_online_update(scores[j], value_fn(c, j), *carry[j]) for j in streams)
        return tuple(_online_update(score_fn(c, j), value_fn(c, j), *carry[j]) for j in streams)

    init = (jnp.full((rows, 1), M_INIT, F32), jnp.zeros((rows, VALUE_SLOT), F32))
    state = lax.fori_loop(0, n_chunks, body, (init,) * n_streams)
    return [_normalize(acc) for _, acc in state]


def _sigmoid(x):
    return 1.0 / (1.0 + jnp.exp(-x))


def _split3(x):
    hi = x.astype(BF16)
    r1 = x - hi.astype(F32)
    mid = r1.astype(BF16)
    lo = (r1 - mid.astype(F32)).astype(BF16)
    return hi, mid, lo


def _pad_rows(rows, total):
    return jnp.concatenate([rows, jnp.zeros((total - rows.shape[0], rows.shape[1]), rows.dtype)], axis=0)


class _Seg:
    def __init__(self, col, width, rope=0, logsig=False, outs=((F32, 1.0),), values_from=None):
        self.col, self.width, self.rope, self.logsig, self.outs = col, width, rope, logsig, outs
        self.values_from = values_from

    def out_width(self, dtype):
        if self.values_from is None or dtype != BF16:
            return self.width
        return self.values_from + VALUE_SLOT * ((self.width - self.values_from) // HEAD_DIM)


def _with_ones(v):
    first = _iota((v.shape[0], LANES), 1) < HEAD_DIM
    blocks = []
    for b in range(v.shape[1] // LANES):
        pair = v[:, b * LANES:(b + 1) * LANES]
        blocks += [jnp.where(first, pair, 1.0), jnp.where(first, pltpu.roll(pair, HEAD_DIM, 1), 1.0)]
    return jnp.concatenate(blocks, axis=1)


def _rope(y, cos, sin):
    w = y.shape[1]
    reps = w // LANES
    cw = jnp.concatenate([cos] * reps, axis=1) if reps > 1 else cos
    sw = jnp.concatenate([sin] * reps, axis=1) if reps > 1 else sin
    half = HEAD_DIM // 2
    first = (_iota(y.shape, 1) & (HEAD_DIM - 1)) < half
    rot = jnp.where(first, pltpu.roll(y, w - half, 1), pltpu.roll(y, half, 1))
    return y * cw + rot * sw


def _log_sigmoid(x):
    return -(jnp.maximum(-x, 0.0) + jnp.log1p(jnp.exp(-jnp.abs(x))))


def _proj_kernel(x_ref, g_ref, w_ref, cos_ref, sin_ref, b_ref, *out_refs, segs):
    x = x_ref[...]
    h = x * lax.rsqrt(jnp.mean(x * x, axis=-1, keepdims=True) + RMS_EPS)
    hb = (h * g_ref[...]).astype(BF16)
    oi = 0
    for seg in segs:
        y = _dot(hb, w_ref[:, seg.col:seg.col + seg.width])
        if seg.rope == seg.width:
            y = _rope(y, cos_ref[...], sin_ref[...])
        elif seg.rope:
            y = jnp.concatenate([_rope(y[:, :seg.rope], cos_ref[...], sin_ref[...]), y[:, seg.rope:]], axis=1)
        if seg.logsig:
            y = _log_sigmoid(y + b_ref[...])
        for dtype, scale in seg.outs:
            out = y if scale == 1.0 else y * scale
            if seg.out_width(dtype) != seg.width:
                out = jnp.concatenate([out[:, :seg.values_from], _with_ones(out[:, seg.values_from:])], axis=1)
            out_refs[oi][...] = out.astype(dtype)
            oi += 1


def _project(x, g, w_bf16, bias, cos, sin, segs, tm):
    n, d = x.shape
    assert n % tm == 0 and cos.shape[0] % tm == 0
    n_pos_tiles = cos.shape[0] // tm
    out_shape, out_specs = [], []
    for seg in segs:
        for dtype, _ in seg.outs:
            out_shape.append(jax.ShapeDtypeStruct((n, seg.out_width(dtype)), dtype))
            out_specs.append(pl.BlockSpec((tm, seg.out_width(dtype)), lambda i: (i, 0)))
    return pl.pallas_call(
        functools.partial(_proj_kernel, segs=segs),
        out_shape=out_shape,
        grid=(n // tm,),
        in_specs=[
            pl.BlockSpec((tm, d), lambda i: (i, 0)),
            pl.BlockSpec((1, d), lambda i: (0, 0)),
            pl.BlockSpec(w_bf16.shape, lambda i: (0, 0)),
            pl.BlockSpec((tm, LANES), lambda i: (i % n_pos_tiles, 0)),
            pl.BlockSpec((tm, LANES), lambda i: (i % n_pos_tiles, 0)),
            pl.BlockSpec((1, LANES), lambda i: (0, 0)),
        ],
        out_specs=out_specs,
        compiler_params=_cparams(1),
        name="rmsnorm_proj",
    )(x, g.reshape(1, d), w_bf16, cos, sin, bias)


def _out_kernel(o_ref, z_ref, x_ref, w_ref, g_ref, y_ref, *yn_ref):
    z = z_ref[...]
    a = o_ref[...] * (z * _sigmoid(z))
    y = x_ref[...] + _dot(a.astype(BF16), w_ref[...])
    y_ref[...] = y
    if yn_ref:
        yn_ref[0][...] = y * lax.rsqrt(jnp.mean(y * y, axis=-1, keepdims=True) + RMS_EPS) * g_ref[...]


def _out_project(o, z, x, w_bf16, final_g, tm):
    n, d = x.shape
    hd = o.shape[1]
    row = lambda w: pl.BlockSpec((tm, w), lambda i: (i, 0))
    final = final_g is not None
    g = final_g.reshape(1, d) if final else jnp.zeros((1, d), F32)
    out_shape = [jax.ShapeDtypeStruct((n, d), F32)] * (2 if final else 1)
    return pl.pallas_call(
        _out_kernel,
        out_shape=out_shape,
        grid=(n // tm,),
        in_specs=[row(hd), row(hd), row(d), pl.BlockSpec((hd, d), lambda i: (0, 0)),
                  pl.BlockSpec((1, d), lambda i: (0, 0))],
        out_specs=[row(d)] * (2 if final else 1),
        compiler_params=_cparams(1),
        name="gated_out_proj",
    )(o, z, x, w_bf16, g)


def _paged_view(cache):
    perm = (0, 1) + tuple(range(3, cache.ndim)) + (2,)
    return jnp.transpose(cache, perm).reshape(cache.shape[0], cache.shape[1], -1, cache.shape[2])


def _page_specs(n_pages, feat, page, layer):
    return [pl.BlockSpec((1, 1, feat, page), lambda b, pt, p=p: (layer, pt[b, p], 0, 0)) for p in range(n_pages)]


def _per_seq(*shape):
    return pl.BlockSpec((1,) + shape, lambda b, pt: (b,) + (0,) * len(shape))


def _blockmean_kernel(x_ref, o_ref, *, n_blocks):
    w = x_ref.shape[1]
    means = jnp.sum(x_ref[...].reshape(n_blocks, NSA_BLOCK, w), axis=1) * (1.0 / NSA_BLOCK)
    o_ref[0] = _pad_rows(means, LANES)


def _blockmean_dense(kvc, batch, seq):
    w = kvc.shape[1]
    n_blocks = seq // NSA_BLOCK
    assert n_blocks % SUBLANES == 0 and n_blocks <= LANES
    return pl.pallas_call(
        functools.partial(_blockmean_kernel, n_blocks=n_blocks),
        out_shape=jax.ShapeDtypeStruct((batch, LANES, w), F32),
        grid=(batch,),
        in_specs=[pl.BlockSpec((seq, w), lambda b: (b, 0))],
        out_specs=pl.BlockSpec((1, LANES, w), lambda b: (b, 0, 0)),
        compiler_params=_cparams(1),
        name="nsa_blockmean",
    )(kvc)


def _blockmean_paged_kernel(pt_ref, *refs, n_pages, n_blocks):
    page_refs, o_ref = refs[:n_pages], refs[n_pages]
    rows_t = jnp.concatenate([r[0, 0] for r in page_refs], axis=1)
    past = rows_t.shape[1]
    n_pad = -(-n_blocks // SUBLANES_BF16) * SUBLANES_BF16
    blk = _iota((n_pad, past), 0)
    member = jnp.where((_iota((n_pad, past), 1) >> _log2(NSA_BLOCK)) == blk, _ones_where(blk < n_blocks), 0.0)
    member = member.astype(BF16)
    hi = rows_t.astype(BF16)
    lo = (rows_t - hi.astype(F32)).astype(BF16)
    sums = _dot_nt(member, hi) + _dot_nt(member, lo)
    o_ref[0] = _pad_rows(sums * (1.0 / NSA_BLOCK), LANES)


def _blockmean_paged(pool_t, layer, page_table, n_blocks):
    _, _, w, page = pool_t.shape
    batch, n_pages = page_table.shape
    assert n_blocks * NSA_BLOCK <= n_pages * page and n_blocks <= LANES
    return pl.pallas_call(
        functools.partial(_blockmean_paged_kernel, n_pages=n_pages, n_blocks=n_blocks),
        out_shape=jax.ShapeDtypeStruct((batch, LANES, w), F32),
        grid_spec=pltpu.PrefetchScalarGridSpec(
            num_scalar_prefetch=1, grid=(batch,),
            in_specs=_page_specs(n_pages, w, page, layer),
            out_specs=_per_seq(LANES, w)),
        compiler_params=_cparams(1),
        name="nsa_blockmean_paged",
    )(page_table, *([pool_t] * n_pages))


def _stack_heads(q, g, rep):
    return jnp.concatenate(
        [q[:, (g * rep + r) * HEAD_DIM:(g * rep + r + 1) * HEAD_DIM] for r in range(rep)], axis=0)


def _rep_rows(a, rep):
    return jnp.concatenate([a] * rep, axis=0)


def _add_per_head(s, bias, rep):
    rows, n = bias.shape
    return (s.reshape(rep, rows, n) + bias[None]).reshape(rep * rows, n)


def _nsa_compressed(qg, cmp_bf, g, t, n_cmp, rows, rep, kv_width):
    half = kv_width // 2
    ck = cmp_bf[:, g * HEAD_DIM:(g + 1) * HEAD_DIM]
    cv = cmp_bf[:, half + g * HEAD_DIM:half + (g + 1) * HEAD_DIM]
    blk = _iota((1, LANES), 1)
    visible = _bias_where(blk < n_cmp, (blk + 1) * NSA_BLOCK - 1 <= t)
    s = _dot_nt(qg, ck) + _rep_rows(visible, rep)
    p, den = _softmax_parts(s)
    p = p / _safe_den(den)
    o_cmp = _dot(p.astype(BF16), cv)
    imp = p[0:rows]
    for r in range(1, rep):
        imp = imp + p[r * rows:(r + 1) * rows]
    return o_cmp, imp


def _nsa_select(imps, t0, rows, n_sel):
    n_grp = len(imps)
    n_pad = -(-n_sel // SUBLANES) * SUBLANES
    width = n_grp * LANES
    imp_t = jnp.concatenate(
        [(imp if rows == LANES else _pad_rows(imp, LANES)).T[0:n_pad, :] for imp in imps], axis=1)
    cur = (t0 + (_iota((1, width), 1) & (LANES - 1))) >> _log2(NSA_BLOCK)
    blk = _iota((n_pad, width), 0)
    score = jnp.where(blk == cur, POS_INF, jnp.where(blk == 0, POS_INF, jnp.where(blk < cur, imp_t, NEG_INF)))
    score = jnp.where(blk < n_sel, score, NEG_INF)
    tiles = [score[SUBLANES * r:SUBLANES * (r + 1)] for r in range(n_pad // SUBLANES)]
    ranks = [jnp.zeros((SUBLANES, width), F32) for _ in tiles]
    sub = _iota((SUBLANES, width), 0)
    for m in range(n_sel):
        sm = tiles[m // SUBLANES][m % SUBLANES:m % SUBLANES + 1, :]
        for r, tile in enumerate(tiles):
            if SUBLANES * r > m:
                beats = _ones_where(sm >= tile)
            elif SUBLANES * (r + 1) <= m:
                beats = _ones_where(sm > tile)
            else:
                loses_tie = _ones_where(sub > m - SUBLANES * r)
                beats = jnp.where(sm > tile, 1.0, jnp.where(sm == tile, loses_tie, 0.0))
            ranks[r] = ranks[r] + beats
    top_n = min(NSA_TOP_N, n_sel)
    sel_t = jnp.concatenate(
        [jnp.where(rank < top_n, _ones_where(tile > NEG_INF), 0.0) for rank, tile in zip(ranks, tiles)]
        + [jnp.zeros((LANES - n_pad, width), F32)], axis=0)
    return [sel_t[:, g * LANES:(g + 1) * LANES].T[0:rows, :] for g in range(n_grp)]


def _expand_blocks(sel, kpos):
    onehot = (kpos >> _log2(NSA_BLOCK)) == _iota((LANES, kpos.shape[1]), 0)
    return _dot(sel.astype(BF16), _ones_where(onehot).astype(BF16))


def _nsa_combine(gates, g, rep, rows, n_heads, o_cmp, o_sel, o_win):
    outs = []
    for r in range(rep):
        h = g * rep + r
        sl = slice(r * rows, (r + 1) * rows)
        outs.append(gates[:, h:h + 1] * o_cmp[sl]
                    + gates[:, n_heads + h:n_heads + h + 1] * o_sel[sl]
                    + gates[:, 2 * n_heads + h:2 * n_heads + h + 1] * o_win[sl])
    return outs


def _nsa_prompt_kernel(q_ref, gl_ref, cmp_ref, ks_ref, kw_ref, o_ref, *, tq, kc, seq, n_heads):
    rep = n_heads // N_KV
    v0 = N_KV * HEAD_DIM
    q0 = pl.program_id(1) * tq
    t = q0 + _iota((tq, 1), 0)
    n_blk = seq // NSA_BLOCK
    q = q_ref[...]
    gates = _sigmoid(gl_ref[...])
    cmp_bf = cmp_ref[0].astype(BF16)
    n_chunks = (q0 + tq + kc - 1) // kc
    win_lo = pl.multiple_of(jnp.maximum(q0 - NSA_WINDOW, 0), tq)
    win_len = NSA_WINDOW + tq
    wpos = win_lo + _iota((1, win_len), 1)
    in_window = _bias_where(wpos <= t, wpos > t - NSA_WINDOW)
    qgs = [_stack_heads(q, g, rep) for g in range(N_KV)]
    cmp_out = [_nsa_compressed(qgs[g], cmp_bf, g, t, n_blk, tq, rep, cmp_bf.shape[1]) for g in range(N_KV)]
    sels = _nsa_select([imp for _, imp in cmp_out], q0, tq, n_blk)

    def sel_scores(c, g):
        k0 = pl.multiple_of(c * kc, kc)
        kpos = k0 + _iota((1, kc), 1)
        allowed = _bias_where(_expand_blocks(sels[g], kpos) > 0.5, kpos <= t)
        return _add_per_head(_dot_nt(qgs[g], ks_ref[pl.ds(k0, kc), g * HEAD_DIM:(g + 1) * HEAD_DIM]), allowed, rep)

    def sel_values(c, g):
        return ks_ref[pl.ds(pl.multiple_of(c * kc, kc), kc), v0 + g * VALUE_SLOT:v0 + (g + 1) * VALUE_SLOT]

    o_sels = _chunked_attention(n_chunks, N_KV, rep * tq, sel_scores, sel_values, scores_first=False)
    outs = []
    for g in range(N_KV):
        qg, o_cmp, o_sel = qgs[g], cmp_out[g][0], o_sels[g]
        kk = kw_ref[pl.ds(win_lo, win_len), g * HEAD_DIM:(g + 1) * HEAD_DIM]
        v1 = kw_ref[pl.ds(win_lo, win_len), v0 + g * VALUE_SLOT:v0 + (g + 1) * VALUE_SLOT]
        p, den = _softmax_parts(_add_per_head(_dot_nt(qg, kk), in_window, rep))
        o_win = _dot(p.astype(BF16), v1[:, 0:HEAD_DIM]) / _safe_den(den)
        outs += _nsa_combine(gates, g, rep, tq, n_heads, o_cmp, o_sel, o_win)
    o_ref[...] = jnp.concatenate(outs, axis=1)


def _nsa_prompt_attention(q_bf, gl, cmp_means, ks_bf, kw_bf, batch, seq, kc=512):
    n, hd = q_bf.shape
    n_heads = hd // HEAD_DIM
    kvw = ks_bf.shape[1]
    tq = LANES
    assert seq % kc == 0 and seq % tq == 0 and NSA_WINDOW % tq == 0 and NSA_WINDOW + tq <= seq
    assert seq // NSA_BLOCK <= LANES and seq % NSA_BLOCK == 0
    nq = seq // tq
    return pl.pallas_call(
        functools.partial(_nsa_prompt_kernel, tq=tq, kc=kc, seq=seq, n_heads=n_heads),
        out_shape=jax.ShapeDtypeStruct((n, hd), F32),
        grid=(batch, nq),
        in_specs=[
            pl.BlockSpec((tq, hd), lambda b, i: (b * nq + i, 0)),
            pl.BlockSpec((tq, LANES), lambda b, i: (b * nq + i, 0)),
            pl.BlockSpec((1, LANES, cmp_means.shape[2]), lambda b, i: (b, 0, 0)),
            pl.BlockSpec((seq, kvw), lambda b, i: (b, 0)),
            pl.BlockSpec((seq, kvw), lambda b, i: (b, 0)),
        ],
        out_specs=pl.BlockSpec((tq, hd), lambda b, i: (b * nq + i, 0)),
        compiler_params=_cparams(2),
        name="nsa_prompt_attention",
    )(q_bf, gl, cmp_means, ks_bf, kw_bf)


def _nsa_decode_kernel(pt_ref, q_ref, gl_ref, cmp_ref, ksn_ref, kwn_ref, win_ref, *refs,
                       n_pages, page, t_new, n_heads):
    page_refs = refs[:n_pages]
    o_ref, kbuf, wbuf = refs[n_pages:]
    rep = n_heads // N_KV
    kv_width, past = kbuf.shape
    half = kv_width // 2
    rows = q_ref.shape[1]
    n_cmp = (past + t_new) // NSA_BLOCK
    n_sel = -(-(past + t_new) // NSA_BLOCK)
    w_state = wbuf.shape[1]
    for p in range(n_pages):
        kbuf[:, p * page:(p + 1) * page] = page_refs[p][0, 0].astype(BF16)
    wbuf[...] = win_ref[0, 0].astype(BF16)
    ks_new = _pad_rows(ksn_ref[0].astype(BF16), LANES)
    kw_new = _pad_rows(kwn_ref[0].astype(BF16), LANES)

    t = past + _iota((rows, 1), 0)
    q = q_ref[0]
    gates = _sigmoid(gl_ref[0])
    cmp_bf = cmp_ref[0].astype(BF16)
    kpos = _iota((1, past), 1)
    npos = past + _iota((1, LANES), 1)
    kpos_w = (past - w_state) + _iota((1, w_state), 1)
    okw = _rep_rows(_bias_where(kpos_w <= t, kpos_w > t - NSA_WINDOW, kpos_w >= 0), rep)
    okw_new = _rep_rows(_bias_where(npos <= t, npos > t - NSA_WINDOW), rep)
    qgs = [_stack_heads(q, g, rep) for g in range(N_KV)]
    cmp_out = [_nsa_compressed(qgs[g], cmp_bf, g, t, n_cmp, rows, rep, kv_width) for g in range(N_KV)]
    sels = _nsa_select([imp for _, imp in cmp_out], past, rows, n_sel)
    outs = []
    for g in range(N_KV):
        qg, sel, o_cmp = qgs[g], sels[g], cmp_out[g][0]
        k_rows = slice(g * HEAD_DIM, (g + 1) * HEAD_DIM)
        v_rows = slice(half + g * HEAD_DIM, half + (g + 1) * HEAD_DIM)
        ok = _rep_rows(_bias_where(_expand_blocks(sel, kpos) > 0.5, kpos <= t), rep)
        ok_new = _rep_rows(_bias_where(_expand_blocks(sel, npos) > 0.5, npos <= t), rep)
        p_a, p_b, den = _softmax_parts2(_dot(qg, kbuf[k_rows, :]) + ok, _dot_nt(qg, ks_new[:, k_rows]) + ok_new)
        o_sel = (_dot_nt(p_a.astype(BF16), kbuf[v_rows, :]) + _dot(p_b.astype(BF16), ks_new[:, v_rows])) / _safe_den(den)
        p_a, p_b, den = _softmax_parts2(_dot(qg, wbuf[k_rows, :]) + okw, _dot_nt(qg, kw_new[:, k_rows]) + okw_new)
        o_win = (_dot_nt(p_a.astype(BF16), wbuf[v_rows, :]) + _dot(p_b.astype(BF16), kw_new[:, v_rows])) / _safe_den(den)
        outs += _nsa_combine(gates, g, rep, rows, n_heads, o_cmp, o_sel, o_win)
    o_ref[0] = jnp.concatenate(outs, axis=1)


def _nsa_decode_attention(q_bf, gl, cmp_means, ks_new, kw_new, win_t, slc_t, layer, page_table, t_new):
    batch, rows, hd = q_bf.shape
    n_heads = hd // HEAD_DIM
    _, _, kvw, page = slc_t.shape
    n_pages = page_table.shape[1]
    past = n_pages * page
    w_state = win_t.shape[3]
    assert past % NSA_BLOCK == 0 and -(-(past + t_new) // NSA_BLOCK) <= LANES and rows <= LANES
    return pl.pallas_call(
        functools.partial(_nsa_decode_kernel, n_pages=n_pages, page=page, t_new=t_new, n_heads=n_heads),
        out_shape=jax.ShapeDtypeStruct((batch, rows, hd), F32),
        grid_spec=pltpu.PrefetchScalarGridSpec(
            num_scalar_prefetch=1, grid=(batch,),
            in_specs=[_per_seq(rows, hd), _per_seq(rows, LANES), _per_seq(LANES, kvw), _per_seq(rows, kvw),
                      _per_seq(rows, kvw),
                      pl.BlockSpec((1, 1, kvw, w_state), lambda b, pt: (layer, b, 0, 0))]
            + _page_specs(n_pages, kvw, page, layer),
            out_specs=_per_seq(rows, hd),
            scratch_shapes=[pltpu.VMEM((kvw, past), BF16), pltpu.VMEM((kvw, w_state), BF16)]),
        compiler_params=_cparams(1),
        name="nsa_decode_attention",
    )(page_table, q_bf, gl, cmp_means, ks_new, kw_new, win_t, *([slc_t] * n_pages))


def _cumsum_rows(x, tri_bf):
    hi, mid, lo = _split3(x)
    return _dot(tri_bf, hi) + _dot(tri_bf, mid) + _dot(tri_bf, lo)


def _cumsum_lanes(x, tri_bf):
    hi, mid, lo = _split3(x)
    return _dot(hi, tri_bf) + _dot(mid, tri_bf) + _dot(lo, tri_bf)


def _lower_tri(n):
    return _ones_where(_iota((n, n), 0) >= _iota((n, n), 1)).astype(BF16)


def _upper_tri(n):
    return _ones_where(_iota((n, n), 0) <= _iota((n, n), 1)).astype(BF16)


def _cumsum_kernel(x_ref, o_ref, *, chunk):
    n = x_ref.shape[1]
    tri = _lower_tri(chunk)
    carry = jnp.zeros((1, x_ref.shape[2]), F32)
    for c in range(n // chunk):
        cs = _cumsum_rows(x_ref[0, c * chunk:(c + 1) * chunk, :], tri) + carry
        o_ref[0, c * chunk:(c + 1) * chunk, :] = cs
        carry = cs[chunk - 1:chunk, :]


def _cumsum_dense(lf, chunk=256):
    batch, seq, h = lf.shape
    assert seq % chunk == 0
    return pl.pallas_call(
        functools.partial(_cumsum_kernel, chunk=chunk),
        out_shape=jax.ShapeDtypeStruct(lf.shape, F32),
        grid=(batch,),
        in_specs=[pl.BlockSpec((1, seq, h), lambda b: (b, 0, 0))],
        out_specs=pl.BlockSpec((1, seq, h), lambda b: (b, 0, 0)),
        compiler_params=_cparams(1),
        name="fox_cumsum",
    )(lf)


def _cumsum_paged_kernel(pt_ref, new_ref, *refs, n_pages, page):
    page_refs = refs[:n_pages]
    ck_ref, cn_ref, cnt_ref = refs[n_pages:]
    n_heads = ck_ref.shape[1]
    tri_u = _upper_tri(page)
    local = [_cumsum_lanes(page_refs[p][0, 0], tri_u) for p in range(n_pages)]
    carry = jnp.zeros((n_heads, 1), F32)
    for p in range(n_pages):
        ck_ref[0, :, p * page:(p + 1) * page] = local[p] + carry
        carry = carry + local[p][:, page - 1:page]
    diag = _iota((n_heads, LANES), 0) == _iota((n_heads, LANES), 1)
    total = jnp.sum(jnp.where(diag, carry, 0.0), axis=0, keepdims=True)
    c_new = _cumsum_rows(_pad_rows(new_ref[0], LANES), _lower_tri(LANES)) + total
    cn_ref[0] = c_new[0:cn_ref.shape[1], :]
    cnt_ref[0] = c_new.T[0:n_heads, :]


def _cumsum_paged(logf_t, layer, page_table, lf_new):
    _, _, h, page = logf_t.shape
    batch, n_pages = page_table.shape
    rows = lf_new.shape[1]
    assert h <= LANES and rows <= LANES
    return pl.pallas_call(
        functools.partial(_cumsum_paged_kernel, n_pages=n_pages, page=page),
        out_shape=[jax.ShapeDtypeStruct((batch, h, n_pages * page), F32),
                   jax.ShapeDtypeStruct((batch, rows, LANES), F32),
                   jax.ShapeDtypeStruct((batch, h, LANES), F32)],
        grid_spec=pltpu.PrefetchScalarGridSpec(
            num_scalar_prefetch=1, grid=(batch,),
            in_specs=[_per_seq(rows, LANES)] + _page_specs(n_pages, h, page, layer),
            out_specs=[_per_seq(h, n_pages * page), _per_seq(rows, LANES), _per_seq(h, LANES)]),
        compiler_params=_cparams(1),
        name="fox_cumsum_paged",
    )(page_table, lf_new, *([logf_t] * n_pages))


def _fox_prompt_kernel(q_ref, k_ref, v_ref, cq_ref, ck_ref, o_ref, *, tq, kc):
    heads = q_ref.shape[1] // HEAD_DIM
    q0 = pl.program_id(2) * tq
    t = q0 + _iota((tq, 1), 0)
    n_chunks = (q0 + tq + kc - 1) // kc
    q = q_ref[...]
    cq = cq_ref[0, 0]

    def scores(c, j):
        k0 = pl.multiple_of(c * kc, kc)
        cols = slice(j * HEAD_DIM, (j + 1) * HEAD_DIM)
        decay = cq[:, j:j + 1] - ck_ref[0, 0, j:j + 1, pl.ds(k0, kc)]
        return _dot_nt(q[:, cols], k_ref[pl.ds(k0, kc), cols]) + jnp.where(k0 + _iota((1, kc), 1) <= t, decay, NEG_INF)

    def values(c, j):
        return v_ref[pl.ds(pl.multiple_of(c * kc, kc), kc), j * VALUE_SLOT:(j + 1) * VALUE_SLOT]

    o_ref[...] = jnp.concatenate(_chunked_attention(n_chunks, heads, tq, scores, values, scores_first=True), axis=1)


def _fox_prompt_attention(q_bf, kv_bf, csum, batch, seq, tq=512, kc=512, heads_per_step=4):
    n, hd = q_bf.shape
    wb = heads_per_step * HEAD_DIM
    wv = heads_per_step * VALUE_SLOT
    n_hb = hd // wb
    nq = seq // tq
    assert seq % tq == 0 and seq % kc == 0 and wb % LANES == 0 and hd % wb == 0 and hd % wv == 0
    c_blocks = csum.reshape(batch, seq, n_hb, heads_per_step)
    cq = jnp.transpose(c_blocks, (0, 2, 1, 3))
    ck = jnp.transpose(c_blocks, (0, 2, 3, 1))
    return pl.pallas_call(
        functools.partial(_fox_prompt_kernel, tq=tq, kc=kc),
        out_shape=jax.ShapeDtypeStruct((n, hd), F32),
        grid=(batch, n_hb, nq),
        in_specs=[
            pl.BlockSpec((tq, wb), lambda b, h, i: (b * nq + i, h)),
            pl.BlockSpec((seq, wb), lambda b, h, i: (b, h)),
            pl.BlockSpec((seq, wv), lambda b, h, i: (b, hd // wv + h)),
            pl.BlockSpec((1, 1, tq, heads_per_step), lambda b, h, i: (b, h, i, 0)),
            pl.BlockSpec((1, 1, heads_per_step, seq), lambda b, h, i: (b, h, 0, 0)),
        ],
        out_specs=pl.BlockSpec((tq, wb), lambda b, h, i: (b * nq + i, h)),
        compiler_params=_cparams(3),
        name="fox_prompt_attention",
    )(q_bf, kv_bf, kv_bf, cq, ck)


def _fox_decode_kernel(pt_ref, q_ref, kvn_ref, cq_ref, cnt_ref, ck_ref, *refs, n_pages, page, n_heads):
    page_refs = refs[:n_pages]
    o_ref, kbuf = refs[n_pages:]
    past = kbuf.shape[1]
    hd = n_heads * HEAD_DIM
    rows = q_ref.shape[1]
    for p in range(n_pages):
        kbuf[:, p * page:(p + 1) * page] = page_refs[p][0, 0].astype(BF16)
    kv_new = _pad_rows(kvn_ref[0].astype(BF16), LANES)
    t = past + _iota((rows, 1), 0)
    visible = _iota((1, past), 1) <= t
    visible_new = past + _iota((1, LANES), 1) <= t
    q = q_ref[0]
    cq = cq_ref[0]
    outs = []
    for h in range(n_heads):
        k_rows = slice(h * HEAD_DIM, (h + 1) * HEAD_DIM)
        v_rows = slice(hd + h * HEAD_DIM, hd + (h + 1) * HEAD_DIM)
        qh = q[:, k_rows]
        s_a = _dot(qh, kbuf[k_rows, :]) + (cq[:, h:h + 1] - ck_ref[0, h:h + 1, :])
        s_b = _dot_nt(qh, kv_new[:, k_rows]) + (cq[:, h:h + 1] - cnt_ref[0, h:h + 1, :])
        p_a, p_b, den = _softmax_parts2(jnp.where(visible, s_a, NEG_INF), jnp.where(visible_new, s_b, NEG_INF))
        outs.append((_dot_nt(p_a.astype(BF16), kbuf[v_rows, :]) + _dot(p_b.astype(BF16), kv_new[:, v_rows]))
                    / _safe_den(den))
    o_ref[0] = jnp.concatenate(outs, axis=1)


def _fox_decode_attention(q_bf, kv_new_bf, c_keys, c_new, c_new_t, kv_t, layer, page_table):
    batch, rows, hd = q_bf.shape
    n_heads = hd // HEAD_DIM
    _, _, kvw, page = kv_t.shape
    n_pages = page_table.shape[1]
    past = n_pages * page
    return pl.pallas_call(
        functools.partial(_fox_decode_kernel, n_pages=n_pages, page=page, n_heads=n_heads),
        out_shape=jax.ShapeDtypeStruct((batch, rows, hd), F32),
        grid_spec=pltpu.PrefetchScalarGridSpec(
            num_scalar_prefetch=1, grid=(batch,),
            in_specs=[_per_seq(rows, hd), _per_seq(rows, kvw), _per_seq(rows, LANES), _per_seq(n_heads, LANES),
                      _per_seq(n_heads, past)] + _page_specs(n_pages, kvw, page, layer),
            out_specs=_per_seq(rows, hd),
            scratch_shapes=[pltpu.VMEM((kvw, past), BF16)]),
        compiler_params=_cparams(1),
        name="fox_decode_attention",
    )(page_table, q_bf, kv_new_bf, c_new, c_new_t, c_keys, *([kv_t] * n_pages))


def _float_of_key(key):
    return lax.bitcast_convert_type(key ^ ((key >> 31) & 0x7FFFFFFF), F32)


def _topk_bias(score_ref, top_k, n_chunks, kc):
    rows = score_ref.shape[0]
    assert kc >= top_k and kc % LANES == 0

    def chunk_sum(fn):
        def body(c, acc):
            k0 = pl.multiple_of(c * kc, kc)
            return acc + _lane_fold(fn(score_ref[:, pl.ds(k0, kc)], k0))
        acc = lax.fori_loop(0, n_chunks, body, jnp.zeros((rows, LANES), F32))
        return jnp.sum(acc, axis=-1, keepdims=True)

    def value_bit(i, key):
        cand = key + (jnp.int32(1) << (31 - i))
        cnt = chunk_sum(lambda sc, k0: _ones_where(sc >= _float_of_key(cand)))
        return jnp.where(cnt >= top_k, cand, key)

    int_min = jnp.full((rows, 1), -2 ** 31, jnp.int32)
    kth_key = lax.fori_loop(0, 32, value_bit, int_min)
    kth = _float_of_key(kth_key)
    nxt = _float_of_key(kth_key + 1)
    nxt = jnp.where(kth_key >= 0, jnp.maximum(nxt, float(np.finfo(np.float32).tiny)), nxt)
    inv_width = 1.0 / (nxt - kth)

    def in_bucket(sc):
        return jnp.where(sc >= kth, _ones_where(sc < nxt), 0.0)

    def resid(sc):
        r = sc - kth
        return jnp.where(r > 0.0, r * inv_width, 0.0)

    few = _ones_where(chunk_sum(lambda sc, k0: _ones_where(sc > NEG_INF)) < top_k)
    need = top_k - chunk_sum(lambda sc, k0: _ones_where(sc >= nxt))
    any_tie = jnp.max(_ones_where(chunk_sum(lambda sc, k0: in_bucket(sc)) > need)) > 0.0
    spread = chunk_sum(lambda sc, k0: in_bucket(sc) * _ones_where(sc > kth))
    refine = jnp.logical_and(any_tie, jnp.max(spread) > 0.0)

    def resid_bit(i, key):
        cand = key + (jnp.int32(1) << (31 - i))
        cnt = chunk_sum(lambda sc, k0: in_bucket(sc) * _ones_where(resid(sc) >= _float_of_key(cand)))
        return jnp.where(cnt >= need, cand, key)

    key2 = lax.fori_loop(0, jnp.where(refine, 32, 0), resid_bit, jnp.where(refine, int_min, 0))
    cut = _float_of_key(key2)
    need_tied = need - chunk_sum(lambda sc, k0: in_bucket(sc) * _ones_where(resid(sc) > cut))
    idx_bits = (score_ref.shape[1] - 1).bit_length()

    def tied(sc):
        return in_bucket(sc) * _ones_where(resid(sc) == cut)

    def index_bit(i, last):
        cand = last + (jnp.int32(1) << (idx_bits - 1 - i))
        cnt = chunk_sum(lambda sc, k0: tied(sc) * _ones_where(k0 + _iota((1, kc), 1) < cand))
        return jnp.where(cnt < need_tied, cand, last)

    last0 = jnp.where(any_tie, 0, score_ref.shape[1]) + jnp.zeros((rows, 1), jnp.int32)
    last = lax.fori_loop(0, jnp.where(any_tie, idx_bits, 0), index_bit, last0)

    def write_bias(c, carry):
        k0 = pl.multiple_of(c * kc, kc)
        sc = score_ref[:, pl.ds(k0, kc)]
        boundary = jnp.where(resid(sc) > cut, 1.0, tied(sc) * _ones_where(k0 + _iota((1, kc), 1) <= last))
        taken = jnp.where(sc >= nxt, 1.0, jnp.where(in_bucket(sc) > 0.5, boundary, few))
        score_ref[:, pl.ds(k0, kc)] = jnp.where(sc > NEG_INF, jnp.where(taken > 0.5, 0.0, NEG_INF), NEG_INF)
        return carry

    lax.fori_loop(0, n_chunks, write_bias, 0)


def _indexer_scores(logits, wi, rows, visible):
    logits = jnp.maximum(logits, 0.0)
    w = wi * (IDX_HEADS ** -0.5)
    score = logits[0:rows] * w[:, 0:1]
    for e in range(1, IDX_HEADS):
        score = score + logits[e * rows:(e + 1) * rows] * w[:, e:e + 1]
    return jnp.where(visible, score, NEG_INF)


def _stack_idx_heads(qi):
    return jnp.concatenate([qi[:, e * IDX_DIM:(e + 1) * IDX_DIM] for e in range(IDX_HEADS)], axis=0)


def _dsa_prompt_kernel(q_ref, qi_ref, wi_ref, ki_ref, kv_ref, o_ref, score_ref, *, tq, kc, seq, n_heads):
    rep = n_heads // N_KV
    v0 = N_KV * HEAD_DIM
    q0 = pl.program_id(1) * tq
    t = q0 + _iota((tq, 1), 0)
    n_chunks = (q0 + tq + kc - 1) // kc
    top_k = min(DSA_TOP_K, seq // 4)
    qs = _stack_idx_heads(qi_ref[...])
    wi = wi_ref[...]

    def score_chunk(c, carry):
        k0 = pl.multiple_of(c * kc, kc)
        kpos = k0 + _iota((1, kc), 1)
        logits = _dot_nt(qs, ki_ref[pl.ds(k0, kc), 0:IDX_DIM])
        score_ref[:, pl.ds(k0, kc)] = _indexer_scores(logits, wi, tq, kpos <= t)
        return carry

    lax.fori_loop(0, n_chunks, score_chunk, 0)
    _topk_bias(score_ref, top_k, n_chunks, kc)
    q = q_ref[...]
    qgs = [_stack_heads(q, g, rep) for g in range(N_KV)]

    def scores(c, g):
        k0 = pl.multiple_of(c * kc, kc)
        return _add_per_head(_dot_nt(qgs[g], kv_ref[pl.ds(k0, kc), g * HEAD_DIM:(g + 1) * HEAD_DIM]),
                             score_ref[:, pl.ds(k0, kc)], rep)

    def values(c, g):
        return kv_ref[pl.ds(pl.multiple_of(c * kc, kc), kc), v0 + g * VALUE_SLOT:v0 + (g + 1) * VALUE_SLOT]

    outs = []
    for o_g in _chunked_attention(n_chunks, N_KV, rep * tq, scores, values, scores_first=False):
        outs += [o_g[r * tq:(r + 1) * tq] for r in range(rep)]
    o_ref[...] = jnp.concatenate(outs, axis=1)


def _dsa_prompt_attention(q_bf, qi_bf, wi, ki_bf, kv_bf, batch, seq, tq=256, kc=512):
    n, hd = q_bf.shape
    n_heads = hd // HEAD_DIM
    kvw = kv_bf.shape[1]
    nq = seq // tq
    assert seq % tq == 0 and seq % kc == 0
    row = lambda w: pl.BlockSpec((tq, w), lambda b, i: (b * nq + i, 0))
    return pl.pallas_call(
        functools.partial(_dsa_prompt_kernel, tq=tq, kc=kc, seq=seq, n_heads=n_heads),
        out_shape=jax.ShapeDtypeStruct((n, hd), F32),
        grid=(batch, nq),
        in_specs=[row(hd), row(qi_bf.shape[1]), row(LANES),
                  pl.BlockSpec((seq, LANES), lambda b, i: (b, 0)),
                  pl.BlockSpec((seq, kvw), lambda b, i: (b, 0))],
        out_specs=row(hd),
        scratch_shapes=[pltpu.VMEM((tq, seq), F32)],
        compiler_params=_cparams(2),
        name="dsa_prompt_attention",
    )(q_bf, qi_bf, wi, ki_bf, kv_bf)


def _dsa_decode_score_kernel(pt_ref, qi_ref, wi_ref, kin_ref, *refs, n_pages):
    page_refs, o_ref = refs[:n_pages], refs[n_pages]
    rows = qi_ref.shape[1]
    k_t = jnp.concatenate([r[0, 0] for r in page_refs], axis=1).astype(BF16)
    past = k_t.shape[1]
    k_new = _pad_rows(kin_ref[0][:, 0:IDX_DIM], LANES)
    t = past + _iota((rows, 1), 0)
    qs = _stack_idx_heads(qi_ref[0])
    wi = wi_ref[0]
    o_ref[0] = jnp.concatenate(
        [_indexer_scores(_dot(qs, k_t), wi, rows, _iota((1, past), 1) <= t),
         _indexer_scores(_dot_nt(qs, k_new), wi, rows, past + _iota((1, LANES), 1) <= t)], axis=1)


def _dsa_decode_scores(qi_bf, wi, ki_new_bf, idx_t, layer, page_table):
    batch, rows, _ = qi_bf.shape
    _, _, kw, page = idx_t.shape
    n_pages = page_table.shape[1]
    s_pad = n_pages * page + LANES
    return pl.pallas_call(
        functools.partial(_dsa_decode_score_kernel, n_pages=n_pages),
        out_shape=jax.ShapeDtypeStruct((batch, rows, s_pad), F32),
        grid_spec=pltpu.PrefetchScalarGridSpec(
            num_scalar_prefetch=1, grid=(batch,),
            in_specs=[_per_seq(rows, qi_bf.shape[2]), _per_seq(rows, LANES), _per_seq(rows, LANES)]
            + _page_specs(n_pages, kw, page, layer),
            out_specs=_per_seq(rows, s_pad)),
        compiler_params=_cparams(1),
        name="dsa_decode_scores",
    )(page_table, qi_bf, wi, ki_new_bf, *([idx_t] * n_pages))


def _select_kernel(s_ref, o_ref, *, top_k):
    o_ref[...] = s_ref[...]
    _topk_bias(o_ref, top_k, 1, o_ref.shape[1])


def _topk_select(scores, top_k, tr=128):
    n, s = scores.shape
    tr = min(tr, n)
    assert n % tr == 0
    return pl.pallas_call(
        functools.partial(_select_kernel, top_k=top_k),
        out_shape=jax.ShapeDtypeStruct((n, s), F32),
        grid=(n // tr,),
        in_specs=[pl.BlockSpec((tr, s), lambda i: (i, 0))],
        out_specs=pl.BlockSpec((tr, s), lambda i: (i, 0)),
        compiler_params=_cparams(1),
        name="dsa_topk_select",
    )(scores)


def _dsa_decode_kernel(pt_ref, q_ref, bias_ref, kvn_ref, *refs, n_pages, page, n_heads):
    page_refs = refs[:n_pages]
    o_ref, kbuf = refs[n_pages:]
    rep = n_heads // N_KV
    kv_width, past = kbuf.shape
    half = kv_width // 2
    rows = q_ref.shape[1]
    for p in range(n_pages):
        kbuf[:, p * page:(p + 1) * page] = page_refs[p][0, 0].astype(BF16)
    kv_new = _pad_rows(kvn_ref[0].astype(BF16), LANES)
    q = q_ref[0]
    bias = _rep_rows(bias_ref[0, :, 0:past], rep)
    bias_new = _rep_rows(bias_ref[0, :, past:past + LANES], rep)
    outs = []
    for g in range(N_KV):
        qg = _stack_heads(q, g, rep)
        k_rows = slice(g * HEAD_DIM, (g + 1) * HEAD_DIM)
        v_rows = slice(half + g * HEAD_DIM, half + (g + 1) * HEAD_DIM)
        p_a, p_b, den = _softmax_parts2(_dot(qg, kbuf[k_rows, :]) + bias, _dot_nt(qg, kv_new[:, k_rows]) + bias_new)
        o_g = (_dot_nt(p_a.astype(BF16), kbuf[v_rows, :]) + _dot(p_b.astype(BF16), kv_new[:, v_rows])) / _safe_den(den)
        outs += [o_g[r * rows:(r + 1) * rows] for r in range(rep)]
    o_ref[0] = jnp.concatenate(outs, axis=1)


def _dsa_decode_attention(q_bf, bias, kv_new_bf, kv_t, layer, page_table):
    batch, rows, hd = q_bf.shape
    n_heads = hd // HEAD_DIM
    _, _, kvw, page = kv_t.shape
    n_pages = page_table.shape[1]
    past = n_pages * page
    assert bias.shape[2] == past + LANES
    return pl.pallas_call(
        functools.partial(_dsa_decode_kernel, n_pages=n_pages, page=page, n_heads=n_heads),
        out_shape=jax.ShapeDtypeStruct((batch, rows, hd), F32),
        grid_spec=pltpu.PrefetchScalarGridSpec(
            num_scalar_prefetch=1, grid=(batch,),
            in_specs=[_per_seq(rows, hd), _per_seq(rows, past + LANES), _per_seq(rows, kvw)]
            + _page_specs(n_pages, kvw, page, layer),
            out_specs=_per_seq(rows, hd),
            scratch_shapes=[pltpu.VMEM((kvw, past), BF16)]),
        compiler_params=_cparams(1),
        name="dsa_decode_attention",
    )(page_table, q_bf, bias, kv_new_bf, *([kv_t] * n_pages))


def _pad_cols(w, width):
    return jnp.pad(w, ((0, 0), (0, width - w.shape[1])))


def _rope_tables(pos):
    half = HEAD_DIM // 2
    inv_freq = ROPE_THETA ** (-jnp.arange(half, dtype=F32) / half)
    ang = pos.astype(F32)[:, None] * inv_freq[None, :]
    cos, sin = jnp.cos(ang), jnp.sin(ang)
    reps = LANES // HEAD_DIM
    return (jnp.concatenate([cos, cos] * reps, axis=1), jnp.concatenate([-sin, sin] * reps, axis=1))


def _nsa_pack(w_in, hd, kvd, n_heads):
    gl0 = hd + 3 * kvd
    gate_cols = np.array([gl0 + h * 3 + c for c in range(3) for h in range(n_heads)])
    w = jnp.concatenate([w_in[:, :gl0], _pad_cols(w_in[:, gate_cols], LANES), w_in[:, gl0 + 3 * n_heads:]], axis=1)
    kv_outs = ((F32, 1.0), (BF16, 1.0))
    segs = [
        _Seg(0, hd, rope=hd, outs=((BF16, HEAD_DIM ** -0.5),)),
        _Seg(hd, kvd, rope=kvd // 2, outs=((F32, 1.0),)),
        _Seg(hd + kvd, kvd, rope=kvd // 2, outs=kv_outs, values_from=kvd // 2),
        _Seg(hd + 2 * kvd, kvd, rope=kvd // 2, outs=kv_outs, values_from=kvd // 2),
        _Seg(gl0, LANES),
        _Seg(gl0 + LANES, hd),
    ]
    return w.astype(BF16), segs


def _fox_pack(w_in, hd, n_heads):
    fl0 = 3 * hd
    w = jnp.concatenate([w_in[:, :fl0], _pad_cols(w_in[:, fl0:fl0 + n_heads], LANES), w_in[:, fl0 + n_heads:]], axis=1)
    segs = [
        _Seg(0, hd, outs=((BF16, HEAD_DIM ** -0.5),)),
        _Seg(hd, 2 * hd, outs=((F32, 1.0), (BF16, 1.0)), values_from=hd),
        _Seg(fl0, LANES, logsig=True),
        _Seg(fl0 + LANES, hd),
    ]
    return w.astype(BF16), segs


def _dsa_pack(w_in, hd, kvd):
    qi0 = hd + kvd
    ki0 = qi0 + IDX_HEADS * IDX_DIM
    wi0 = ki0 + IDX_DIM
    z0 = wi0 + IDX_HEADS
    w = jnp.concatenate([w_in[:, :ki0], _pad_cols(w_in[:, ki0:wi0], LANES), _pad_cols(w_in[:, wi0:z0], LANES),
                         w_in[:, z0:]], axis=1)
    segs = [
        _Seg(0, hd, rope=hd, outs=((BF16, HEAD_DIM ** -0.5),)),
        _Seg(hd, kvd, rope=kvd // 2, outs=((F32, 1.0), (BF16, 1.0)), values_from=kvd // 2),
        _Seg(qi0, IDX_HEADS * IDX_DIM, rope=IDX_HEADS * IDX_DIM, outs=((BF16, IDX_DIM ** -0.5),)),
        _Seg(ki0, LANES, rope=LANES, outs=((F32, 1.0), (BF16, 1.0))),
        _Seg(ki0 + LANES, LANES),
        _Seg(ki0 + 2 * LANES, hd),
    ]
    return w.astype(BF16), segs


def kernel(x_prompt, x_sample, cache_nsa_cmp_kv, cache_nsa_slc_kv, state_nsa_win_kv, cache_fox_kv, cache_fox_logf,
           cache_dsa_kv, cache_dsa_idx_k, page_table, norm_g, final_norm_g, nsa_w_in, nsa_w_out, fox_w_in, fox_b_f,
           fox_w_out, dsa_w_in, dsa_w_out):
    batch, seq, d_model = x_prompt.shape
    dec_batch, t_new, _ = x_sample.shape
    depth = norm_g.shape[0]
    hd = nsa_w_out.shape[1]
    n_heads = hd // HEAD_DIM
    kvd = 2 * N_KV * HEAD_DIM
    n_pages = page_table.shape[1]
    page = cache_fox_kv.shape[2]
    past = n_pages * page
    assert t_new <= ROWS_PAD and seq % NSA_BLOCK == 0
    tm = 256

    xp = x_prompt.reshape(batch * seq, d_model)
    xs = jnp.pad(x_sample, ((0, 0), (0, ROWS_PAD - t_new), (0, 0))).reshape(dec_batch * ROWS_PAD, d_model)
    cos_p, sin_p = _rope_tables(jnp.arange(seq, dtype=jnp.int32))
    tm_s = min(tm, dec_batch * ROWS_PAD)
    pos_s = past + (jnp.arange(tm_s, dtype=jnp.int32) % ROWS_PAD)
    cos_s, sin_s = _rope_tables(pos_s)
    zero_bias = jnp.zeros((1, LANES), F32)
    per_seq = lambda a: a.reshape(dec_batch, ROWS_PAD, a.shape[-1])
    trim = lambda a: per_seq(a)[:, :t_new]
    kv6 = lambda a, b: a.reshape(b, -1, 2, N_KV, HEAD_DIM)

    cmp_t, slc_t, win_t = _paged_view(cache_nsa_cmp_kv), _paged_view(cache_nsa_slc_kv), _paged_view(state_nsa_win_kv)
    fox_kv_t, fox_lf_t = _paged_view(cache_fox_kv), _paged_view(cache_fox_logf)
    dsa_kv_t, dsa_ik_t = _paged_view(cache_dsa_kv), _paged_view(cache_dsa_idx_k)

    outs = {k: [] for k in ("nsa_c_p", "nsa_c_s", "nsa_s_p", "nsa_s_s", "nsa_w_p", "nsa_w_new", "fox_kv_p", "fox_kv_s",
                            "fox_lf_p", "fox_lf_s", "dsa_kv_p", "dsa_kv_s", "dsa_ik_p", "dsa_ik_s")}
    yp_final = ys_final = None
    for i in range(depth):
        kind, j = i % N_MIXERS, i // N_MIXERS
        g = norm_g[i]
        last = i == depth - 1
        if kind == 0:
            w, segs = _nsa_pack(nsa_w_in[j], hd, kvd, n_heads)
            w_out = nsa_w_out[j].astype(BF16)
            q, kvc, kvs, kvs_b, kvw, kvw_b, gl, z = _project(xp, g, w, zero_bias, cos_p, sin_p, segs, tm)
            cmp_means = _blockmean_dense(kvc, batch, seq)
            o_p = _nsa_prompt_attention(q, gl, cmp_means, kvs_b, kvw_b, batch, seq)
            z_p = z
            outs["nsa_c_p"].append(kv6(kvc, batch))
            outs["nsa_s_p"].append(kv6(kvs, batch))
            outs["nsa_w_p"].append(kv6(kvw, batch)[:, seq - min(NSA_WINDOW, seq):])

            q, kvc, kvs, kvs_b, kvw, kvw_b, gl, z = _project(xs, g, w, zero_bias, cos_s, sin_s, segs, tm_s)
            cmp_means = _blockmean_paged(cmp_t, j, page_table, (past + t_new) // NSA_BLOCK)
            o_s = _nsa_decode_attention(per_seq(q), per_seq(gl), cmp_means, per_seq(kvs), per_seq(kvw), win_t,
                                        slc_t, j, page_table, t_new)
            o_s = o_s.reshape(dec_batch * ROWS_PAD, hd)
            z_s = z
            outs["nsa_c_s"].append(kv6(trim(kvc), dec_batch))
            outs["nsa_s_s"].append(kv6(trim(kvs), dec_batch))
            outs["nsa_w_new"].append(kv6(trim(kvw), dec_batch))
        elif kind == 1:
            w, segs = _fox_pack(fox_w_in[j], hd, n_heads)
            w_out = fox_w_out[j].astype(BF16)
            bias = _pad_cols(fox_b_f[j].reshape(1, n_heads), LANES)
            q, kv, kv_b, lf, z = _project(xp, g, w, bias, cos_p, sin_p, segs, tm)
            lf_p = lf[:, :n_heads].reshape(batch, seq, n_heads)
            o_p = _fox_prompt_attention(q, kv_b, _cumsum_dense(lf_p), batch, seq)
            z_p = z
            outs["fox_kv_p"].append(kv.reshape(batch, seq, 2, n_heads, HEAD_DIM))
            outs["fox_lf_p"].append(lf_p)

            q, kv, kv_b, lf, z = _project(xs, g, w, bias, cos_s, sin_s, segs, tm_s)
            c_keys, c_new, c_new_t = _cumsum_paged(fox_lf_t, j, page_table, per_seq(lf))
            o_s = _fox_decode_attention(per_seq(q), per_seq(kv), c_keys, c_new, c_new_t, fox_kv_t, j, page_table)
            o_s = o_s.reshape(dec_batch * ROWS_PAD, hd)
            z_s = z
            outs["fox_kv_s"].append(trim(kv).reshape(dec_batch, t_new, 2, n_heads, HEAD_DIM))
            outs["fox_lf_s"].append(trim(lf)[:, :, :n_heads])
        else:
            w, segs = _dsa_pack(dsa_w_in[j], hd, kvd)
            w_out = dsa_w_out[j].astype(BF16)
            q, kv, kv_b, qi, ki, ki_b, wi, z = _project(xp, g, w, zero_bias, cos_p, sin_p, segs, tm)
            o_p = _dsa_prompt_attention(q, qi, wi, ki_b, kv_b, batch, seq)
            z_p = z
            outs["dsa_kv_p"].append(kv.reshape(batch, seq, 2, N_KV, HEAD_DIM))
            outs["dsa_ik_p"].append(ki[:, :IDX_DIM].reshape(batch, seq, IDX_DIM))

            q, kv, kv_b, qi, ki, ki_b, wi, z = _project(xs, g, w, zero_bias, cos_s, sin_s, segs, tm_s)
            scores = _dsa_decode_scores(per_seq(qi), per_seq(wi), per_seq(ki_b), dsa_ik_t, j, page_table)
            s_pad = scores.shape[2]
            sel_bias = _topk_select(scores.reshape(dec_batch * ROWS_PAD, s_pad), min(DSA_TOP_K, (past + t_new) // 4))
            o_s = _dsa_decode_attention(per_seq(q), sel_bias.reshape(dec_batch, ROWS_PAD, s_pad), per_seq(kv),
                                        dsa_kv_t, j, page_table)
            o_s = o_s.reshape(dec_batch * ROWS_PAD, hd)
            z_s = z
            outs["dsa_kv_s"].append(trim(kv).reshape(dec_batch, t_new, 2, N_KV, HEAD_DIM))
            outs["dsa_ik_s"].append(trim(ki)[:, :, :IDX_DIM])
        fg = final_norm_g if last else None
        res_p = _out_project(o_p, z_p, xp, w_out, fg, tm)
        res_s = _out_project(o_s, z_s, xs, w_out, fg, tm_s)
        xp, xs = res_p[0], res_s[0]
        if last:
            yp_final, ys_final = res_p[1], res_s[1]

    y_prompt = yp_final.reshape(batch, seq, d_model)
    y_sample = ys_final.reshape(dec_batch, ROWS_PAD, d_model)[:, :t_new]
    st = jnp.stack
    keys_w = jnp.concatenate([state_nsa_win_kv, st(outs["nsa_w_new"])], axis=2)
    win_s = keys_w[:, :, keys_w.shape[2] - min(NSA_WINDOW, keys_w.shape[2]):]
    return (y_prompt, y_sample,
            st(outs["nsa_c_p"]), st(outs["nsa_c_s"]), st(outs["nsa_s_p"]), st(outs["nsa_s_s"]),
            st(outs["nsa_w_p"]), win_s,
            st(outs["fox_kv_p"]), st(outs["fox_kv_s"]), st(outs["fox_lf_p"]), st(outs["fox_lf_s"]),
            st(outs["dsa_kv_p"]), st(outs["dsa_kv_s"]), st(outs["dsa_ik_p"]), st(outs["dsa_ik_s"]))
```
